```python
import jax, jax.numpy as jnp
from jax import lax

D_MODEL = 1024
BATCH = 8
SEQ = 4096
DEPTH = 4

N_HEADS = 16
HEAD_DIM = 128
D_INNER = N_HEADS * HEAD_DIM
Q_BLOCK = 128
N_MIXERS = 2
LN_EPS = 1e-5
DEEPNORM_ALPHA = (2 * DEPTH) ** 0.25
DEEPNORM_BETA = (8 * DEPTH) ** -0.25
FOX_COLS = 4 * D_INNER + N_HEADS
SB_COLS = 4 * D_INNER
N_FOX = (DEPTH + 1) // 2
N_SB = DEPTH // 2

kernel_name = "fox_stickbreaking_interleaved_deepnorm"


def _to_blocks(t):
    b, s = t.shape[:2]
    t = t.reshape((b, s // Q_BLOCK, Q_BLOCK) + t.shape[2:])
    return jnp.moveaxis(t, 1, 0)


def _from_blocks(t):
    nb, b, qb = t.shape[:3]
    return jnp.moveaxis(t, 0, 1).reshape(b, nb * qb, -1)


def forgetting_attention(q, k, v, log_f):
    s = q.shape[1]
    scale = HEAD_DIM ** -0.5
    cum = jnp.cumsum(log_f, axis=1)
    cum_k = jnp.transpose(cum, (0, 2, 1))
    k_pos = jnp.arange(s)

    def one_block(args):
        q_blk, c_blk, blk = args
        q_pos = blk * Q_BLOCK + jnp.arange(Q_BLOCK)
        logits = jnp.einsum('bqhd,bkhd->bhqk', q_blk, k) * scale
        logits = logits + jnp.transpose(c_blk, (0, 2, 1))[..., None] - cum_k[:, :, None, :]
        causal = k_pos[None, :] <= q_pos[:, None]
        logits = jnp.where(causal, logits, -jnp.inf)
        p = jax.nn.softmax(logits, axis=-1)
        return jnp.einsum('bhqk,bkhd->bqhd', p, v)

    nb = s // Q_BLOCK
    out = lax.map(one_block, (_to_blocks(q), _to_blocks(cum), jnp.arange(nb)))
    return _from_blocks(out)


def stick_breaking_attention(q, k, v):
    s = q.shape[1]
    scale = HEAD_DIM ** -0.5
    k_pos = jnp.arange(s)

    def one_block(args):
        q_blk, blk = args
        q_pos = blk * Q_BLOCK + jnp.arange(Q_BLOCK)
        z = jnp.einsum('bqhd,bkhd->bhqk', q_blk, k) * scale
        strict = k_pos[None, :] < q_pos[:, None]
        log_one_minus = jnp.where(strict, jax.nn.log_sigmoid(-z), 0.0)
        remain = lax.cumsum(log_one_minus, axis=3, reverse=True) - log_one_minus
        weights = jnp.where(strict, jnp.exp(jax.nn.log_sigmoid(z) + remain), 0.0)
        return jnp.einsum('bhqk,bkhd->bqhd', weights, v)

    nb = s // Q_BLOCK
    out = lax.map(one_block, (_to_blocks(q), jnp.arange(nb)))
    return _from_blocks(out)


def _layer_norm(x, g, b):
    xf = x.astype(jnp.float32)
    mu = jnp.mean(xf, axis=-1, keepdims=True)
    var = jnp.mean(jnp.square(xf - mu), axis=-1, keepdims=True)
    y = (xf - mu) * lax.rsqrt(var + LN_EPS) * g.astype(jnp.float32) + b.astype(jnp.float32)
    return y.astype(x.dtype)


def setup_inputs(seed: int = 0) -> dict:
    key = jax.random.key(seed)
    ks = jax.random.split(key, 8)
    x = jax.random.normal(ks[0], (BATCH, SEQ, D_MODEL), jnp.float32)

    fox_scale = jnp.ones((FOX_COLS,), jnp.float32).at[2 * D_INNER:3 * D_INNER].set(DEEPNORM_BETA)
    sb_scale = jnp.ones((SB_COLS,), jnp.float32).at[2 * D_INNER:3 * D_INNER].set(DEEPNORM_BETA)
    fox_w_in = jax.random.normal(ks[1], (N_FOX, D_MODEL, FOX_COLS), jnp.float32) * (D_MODEL ** -0.5) * fox_scale
    sb_w_in = jax.random.normal(ks[2], (N_SB, D_MODEL, SB_COLS), jnp.float32) * (D_MODEL ** -0.5) * sb_scale
    fox_b_f = jax.random.uniform(ks[3], (N_FOX, N_HEADS), jnp.float32, minval=1.0, maxval=4.0)
    out_std = (D_INNER ** -0.5) * DEEPNORM_BETA
    fox_w_out = jax.random.normal(ks[4], (N_FOX, D_INNER, D_MODEL), jnp.float32) * out_std
    sb_w_out = jax.random.normal(ks[5], (N_SB, D_INNER, D_MODEL), jnp.float32) * out_std
    ln_g = 1.0 + 0.02 * jax.random.normal(ks[6], (DEPTH, D_MODEL), jnp.float32)
    ln_b = 0.02 * jax.random.normal(ks[7], (DEPTH, D_MODEL), jnp.float32)
    return {"x": x, "fox_w_in": fox_w_in, "fox_b_f": fox_b_f, "fox_w_out": fox_w_out,
            "sb_w_in": sb_w_in, "sb_w_out": sb_w_out, "ln_g": ln_g, "ln_b": ln_b}


def reference(x, fox_w_in, fox_b_f, fox_w_out, sb_w_in, sb_w_out, ln_g, ln_b):
    b, s, _ = x.shape
    for layer in range(DEPTH):
        slot = layer // N_MIXERS
        use_fox = (layer % N_MIXERS) == 0
        w_in = fox_w_in[slot] if use_fox else sb_w_in[slot]
        w_out = fox_w_out[slot] if use_fox else sb_w_out[slot]

        h = x @ w_in
        q = h[..., 0 * D_INNER:1 * D_INNER].reshape(b, s, N_HEADS, HEAD_DIM).astype(jnp.float32)
        k = h[..., 1 * D_INNER:2 * D_INNER].reshape(b, s, N_HEADS, HEAD_DIM).astype(jnp.float32)
        v = h[..., 2 * D_INNER:3 * D_INNER].reshape(b, s, N_HEADS, HEAD_DIM).astype(jnp.float32)
        z = h[..., 3 * D_INNER:4 * D_INNER]

        if use_fox:
            f_logit = h[..., 4 * D_INNER:].astype(jnp.float32)
            log_f = jax.nn.log_sigmoid(f_logit + fox_b_f[slot].astype(jnp.float32))
            o = forgetting_attention(q, k, v, log_f)
        else:
            o = stick_breaking_attention(q, k, v)

        y = (o.astype(x.dtype) * jax.nn.silu(z)) @ w_out
        x = _layer_norm(DEEPNORM_ALPHA * x + y, ln_g[layer], ln_b[layer])
    return x
```

```python
import functools
import math

import jax
import jax.numpy as jnp
from jax import lax
from jax.experimental import pallas as pl
from jax.experimental.pallas import tpu as pltpu

LANES = 128
LN_EPS = 1e-5
NEG_BIG = -1e30
VMEM_LIMIT = 56 * 1024 * 1024

_NT = (((1,), (1,)), ((), ()))


def _params(*sem):
    return pltpu.CompilerParams(dimension_semantics=sem, vmem_limit_bytes=VMEM_LIMIT)


def _gates_kernel(x_ref, wf_ref, bf_ref, tri_ref, c_ref, carry_ref):
    @pl.when(pl.program_id(1) == 0)
    def _():
        carry_ref[...] = jnp.zeros_like(carry_ref)

    f = jnp.dot(x_ref[0], wf_ref[...], preferred_element_type=jnp.float32,
                precision=lax.Precision.HIGHEST) + bf_ref[...]
    log_f = jnp.minimum(f, 0.0) - jnp.log(1.0 + jnp.exp(-jnp.abs(f)))
    c = jnp.dot(tri_ref[...], log_f, preferred_element_type=jnp.float32,
                precision=lax.Precision.HIGHEST) + carry_ref[...]
    c_ref[0] = c
    carry_ref[...] = c[-1:, :]


def _fox_gates(x, wf, bf, ts=512):
    b, s, d = x.shape
    tri = (jnp.arange(ts)[:, None] >= jnp.arange(ts)[None, :]).astype(jnp.float32)
    return pl.pallas_call(
        _gates_kernel,
        grid=(b, s // ts),
        in_specs=[pl.BlockSpec((1, ts, d), lambda i, j: (i, j, 0)),
                  pl.BlockSpec((d, LANES), lambda i, j: (0, 0)),
                  pl.BlockSpec((1, LANES), lambda i, j: (0, 0)),
                  pl.BlockSpec((ts, ts), lambda i, j: (0, 0))],
        out_specs=pl.BlockSpec((1, ts, LANES), lambda i, j: (i, j, 0)),
        out_shape=jax.ShapeDtypeStruct((b, s, LANES), jnp.float32),
        scratch_shapes=[pltpu.VMEM((1, LANES), jnp.float32)],
        compiler_params=_params("parallel", "arbitrary"),
        name="fox_gates",
    )(x, wf, bf, tri)


def _inproj_kernel(x_ref, w_ref, o_ref, xb_ref, *, q_tiles, scale, heads_per_tile, dh):
    n = pl.program_id(2)

    @pl.when(n == 0)
    def _():
        xb_ref[...] = x_ref[0].astype(jnp.bfloat16)

    acc = jnp.dot(xb_ref[...], w_ref[...], preferred_element_type=jnp.float32)
    acc = acc * jnp.where(n < q_tiles, scale, 1.0)
    for hh in range(heads_per_tile):
        o_ref[0, 0, hh] = acc[:, hh * dh:(hh + 1) * dh].astype(o_ref.dtype)


def _in_proj(x, w, n_heads, dh, scale, tm=1024, tn=512):
    b, s, d = x.shape
    d_inner = n_heads * dh
    tiles_per_section = d_inner // tn
    hpt = tn // dh
    kern = functools.partial(_inproj_kernel, q_tiles=tiles_per_section, scale=scale,
                             heads_per_tile=hpt, dh=dh)
    return pl.pallas_call(
        kern,
        grid=(b, s // tm, 4 * tiles_per_section),
        in_specs=[pl.BlockSpec((1, tm, d), lambda bi, i, n: (bi, i, 0)),
                  pl.BlockSpec((d, tn), lambda bi, i, n: (0, n))],
        out_specs=pl.BlockSpec(
            (1, 1, hpt, tm, dh),
            lambda bi, i, n: (n // tiles_per_section, bi, n % tiles_per_section, i, 0)),
        out_shape=jax.ShapeDtypeStruct((4, b, n_heads, s, dh), jnp.bfloat16),
        scratch_shapes=[pltpu.VMEM((tm, d), jnp.bfloat16)],
        compiler_params=_params("parallel", "parallel", "arbitrary"),
        name="in_proj",
    )(x, w)


def _gate_and_store(o, z_ref, o_ref):
    z = z_ref[0, 0, 0].astype(jnp.float32)
    silu = z * (1.0 / (1.0 + jnp.exp(-z)))
    o_ref[0] = (o * silu).astype(o_ref.dtype)


def _fox_kernel(q_ref, k_ref, v_ref, z_ref, ck_ref, o_ref, vaug_ref, *, t, dh):
    i = pl.program_id(2)

    @pl.when(i == 0)
    def _():
        col = lax.broadcasted_iota(jnp.int32, (v_ref.shape[3], dh), 1)
        vaug_ref[:, :dh] = v_ref[0, 0, 0]
        vaug_ref[:, dh:] = jnp.where(col == 0, 1.0, 0.0).astype(vaug_ref.dtype)

    q = q_ref[0, 0, 0]
    q0 = pl.multiple_of(i * t, t)
    c_ref = ck_ref[0, 0, :, pl.ds(q0, LANES)][:, 0:1]

    def block(j, carry, masked):
        m, acc = carry
        k0 = pl.multiple_of(j * t, t)
        s = lax.dot_general(q, k_ref[0, 0, 0, pl.ds(k0, t), :], _NT,
                            preferred_element_type=jnp.float32)
        s = s - (ck_ref[0, 0, :, pl.ds(k0, t)] - c_ref)
        if masked:
            row = lax.broadcasted_iota(jnp.int32, (t, t), 0)
            col = lax.broadcasted_iota(jnp.int32, (t, t), 1)
            s = jnp.where(col <= row, s, NEG_BIG)
        m_new = jnp.maximum(m, jnp.max(s, axis=-1, keepdims=True))
        p = jnp.exp(s - m_new)
        acc = acc * jnp.exp(m - m_new) + jnp.dot(
            p.astype(jnp.bfloat16), vaug_ref[pl.ds(k0, t), :], preferred_element_type=jnp.float32)
        return m_new, acc

    init = (jnp.full((t, 1), NEG_BIG, jnp.float32), jnp.zeros((t, 2 * dh), jnp.float32))
    carry = lax.fori_loop(0, i, lambda j, c: block(j, c, False), init)
    _, acc = block(i, carry, True)
    o = acc[:, :dh] / acc[:, dh:dh + 1]
    _gate_and_store(o, z_ref, o_ref)


def _sb_kernel(q_ref, k_ref, v_ref, z_ref, u_ref, o_ref, *, t, dh):
    i = pl.program_id(2)
    q = q_ref[0, 0, 0]
    u = u_ref[...]

    def block(j, carry, masked):
        tail, acc = carry
        k0 = pl.multiple_of(j * t, t)
        z = lax.dot_general(q, k_ref[0, 0, 0, pl.ds(k0, t), :], _NT,
                            preferred_element_type=jnp.float32)
        sp = jnp.maximum(z, 0.0) + jnp.log(1.0 + jnp.exp(-jnp.abs(z)))
        if masked:
            row = lax.broadcasted_iota(jnp.int32, (t, t), 0)
            col = lax.broadcasted_iota(jnp.int32, (t, t), 1)
            strict = col < row
            sp = jnp.where(strict, sp, 0.0)
        hi = sp.astype(jnp.bfloat16)
        lo = (sp - hi.astype(jnp.float32)).astype(jnp.bfloat16)
        r = (jnp.dot(hi, u, preferred_element_type=jnp.float32)
             + jnp.dot(lo, u, preferred_element_type=jnp.float32))
        w = jnp.exp(z - r - tail)
        if masked:
            w = jnp.where(strict, w, 0.0)
        acc = acc + jnp.dot(w.astype(jnp.bfloat16), v_ref[0, 0, 0, pl.ds(k0, t), :],
                            preferred_element_type=jnp.float32)
        return tail + r[:, 0:1], acc

    init = (jnp.zeros((t, 1), jnp.float32), jnp.zeros((t, dh), jnp.float32))
    carry = block(i, init, True)
    _, acc = lax.fori_loop(0, i, lambda n, c: block(i - 1 - n, c, False), carry)
    _gate_and_store(acc, z_ref, o_ref)


def _mixer(qkvz, extra, *, fox, t):
    _, b, h, s, dh = qkvz.shape
    tile = lambda sec: pl.BlockSpec((1, 1, 1, t, dh), lambda bi, hi, i: (sec, bi, hi, i, 0))
    full = lambda sec: pl.BlockSpec((1, 1, 1, s, dh), lambda bi, hi, i: (sec, bi, hi, 0, 0))
    if fox:
        kern = functools.partial(_fox_kernel, t=t, dh=dh)
        extra_spec = pl.BlockSpec((1, 1, 1, s), lambda bi, hi, i: (bi, hi, 0, 0))
        scratch = [pltpu.VMEM((s, 2 * dh), jnp.bfloat16)]
    else:
        kern = functools.partial(_sb_kernel, t=t, dh=dh)
        extra_spec = pl.BlockSpec((t, t), lambda bi, hi, i: (0, 0))
        scratch = []
    return pl.pallas_call(
        kern,
        grid=(b, h, s // t),
        in_specs=[tile(0), full(1), full(2), tile(3), extra_spec],
        out_specs=pl.BlockSpec((1, t, dh), lambda bi, hi, i: (bi, i, hi)),
        out_shape=jax.ShapeDtypeStruct((b, s, h * dh), jnp.bfloat16),
        scratch_shapes=scratch,
        compiler_params=_params("parallel", "parallel", "arbitrary"),
        name="fox_mixer" if fox else "sb_mixer",
    )(qkvz, qkvz, qkvz, qkvz, extra)


def _outproj_kernel(g_ref, w_ref, x_ref, lg_ref, lb_ref, o_ref, *, alpha):
    y = jnp.dot(g_ref[0], w_ref[...], preferred_element_type=jnp.float32)
    r = alpha * x_ref[0] + y
    mu = jnp.mean(r, axis=-1, keepdims=True)
    d = r - mu
    var = jnp.mean(d * d, axis=-1, keepdims=True)
    o_ref[0] = d * lax.rsqrt(var + LN_EPS) * lg_ref[...] + lb_ref[...]


def _out_proj_norm(g, w, x, ln_g, ln_b, alpha, tm=512):
    b, s, d = x.shape
    d_inner = g.shape[-1]
    return pl.pallas_call(
        functools.partial(_outproj_kernel, alpha=alpha),
        grid=(b, s // tm),
        in_specs=[pl.BlockSpec((1, tm, d_inner), lambda bi, i: (bi, i, 0)),
                  pl.BlockSpec((d_inner, d), lambda bi, i: (0, 0)),
                  pl.BlockSpec((1, tm, d), lambda bi, i: (bi, i, 0)),
                  pl.BlockSpec((1, d), lambda bi, i: (0, 0)),
                  pl.BlockSpec((1, d), lambda bi, i: (0, 0))],
        out_specs=pl.BlockSpec((1, tm, d), lambda bi, i: (bi, i, 0)),
        out_shape=jax.ShapeDtypeStruct((b, s, d), jnp.float32),
        compiler_params=_params("parallel", "parallel"),
        name="out_proj_norm",
    )(g, w, x, ln_g, ln_b)


def kernel(x, fox_w_in, fox_b_f, fox_w_out, sb_w_in, sb_w_out, ln_g, ln_b):
    b, s, d = x.shape
    depth = ln_g.shape[0]
    n_heads = fox_b_f.shape[1]
    d_inner = fox_w_out.shape[1]
    dh = d_inner // n_heads
    scale = dh ** -0.5
    alpha = (2 * depth) ** 0.25
    t = min(512, s)

    u = (jnp.arange(t)[:, None] >= jnp.arange(t)[None, :]).astype(jnp.bfloat16)
    for layer in range(depth):
        slot = layer // 2
        fox = layer % 2 == 0
        w_in = fox_w_in[slot] if fox else sb_w_in[slot]
        w_out = fox_w_out[slot] if fox else sb_w_out[slot]
        qkvz = _in_proj(x, w_in[:, :4 * d_inner].astype(jnp.bfloat16), n_heads, dh, scale,
                        tm=min(1024, s), tn=min(512, d_inner))
        if fox:
            wf = jnp.pad(w_in[:, 4 * d_inner:], ((0, 0), (0, LANES - n_heads)))
            bf = jnp.pad(fox_b_f[slot], (0, LANES - n_heads))[None, :]
            c = _fox_gates(x, wf, bf, ts=min(512, s))
            ck = jnp.transpose(c[:, :, :n_heads], (0, 2, 1))[:, :, None, :]
            g = _mixer(qkvz, ck, fox=True, t=t)
        else:
            g = _mixer(qkvz, u, fox=False, t=t)
        x = _out_proj_norm(g, w_out.astype(jnp.bfloat16), x, ln_g[layer][None, :],
                           ln_b[layer][None, :], alpha, tm=min(512, s))
    return x
```

```python
import functools
import math

import jax
import jax.numpy as jnp
from jax import lax
from jax.experimental import pallas as pl
from jax.experimental.pallas import tpu as pltpu

LANES = 128
MXU_DIM = 256
LN_EPS = 1e-5
NEG_BIG = -1e30
LOG2E = math.log2(math.e)
VMEM_LIMIT = 56 * 1024 * 1024

_NT = (((1,), (1,)), ((), ()))


def _params(*sem):
    return pltpu.CompilerParams(dimension_semantics=sem, vmem_limit_bytes=VMEM_LIMIT)


def _gates_kernel(x_ref, wf_ref, bf_ref, tri_ref, c_ref, carry_ref):
    @pl.when(pl.program_id(1) == 0)
    def _():
        carry_ref[...] = jnp.zeros_like(carry_ref)

    f = jnp.dot(x_ref[0], wf_ref[...], preferred_element_type=jnp.float32,
                precision=lax.Precision.HIGHEST) + bf_ref[...]
    log_f = jnp.minimum(f, 0.0) - jnp.log(1.0 + jnp.exp(-jnp.abs(f)))
    c = jnp.dot(tri_ref[...], log_f, preferred_element_type=jnp.float32,
                precision=lax.Precision.HIGHEST) + carry_ref[...]
    c_ref[0] = c
    carry_ref[...] = c[-1:, :]


def _fox_gates(x, wf, bf, ts=512):
    b, s, d = x.shape
    tri = (jnp.arange(ts)[:, None] >= jnp.arange(ts)[None, :]).astype(jnp.float32)
    return pl.pallas_call(
        _gates_kernel,
        grid=(b, s // ts),
        in_specs=[pl.BlockSpec((1, ts, d), lambda i, j: (i, j, 0)),
                  pl.BlockSpec((d, LANES), lambda i, j: (0, 0)),
                  pl.BlockSpec((1, LANES), lambda i, j: (0, 0)),
                  pl.BlockSpec((ts, ts), lambda i, j: (0, 0))],
        out_specs=pl.BlockSpec((1, ts, LANES), lambda i, j: (i, j, 0)),
        out_shape=jax.ShapeDtypeStruct((b, s, LANES), jnp.float32),
        scratch_shapes=[pltpu.VMEM((1, LANES), jnp.float32)],
        compiler_params=_params("parallel", "arbitrary"),
        name="fox_gates",
    )(x, wf, bf, tri)


def _inproj_kernel(x_ref, w_ref, o_ref, xb_ref, *, q_tiles, scale, heads_per_tile, dh):
    n = pl.program_id(2)

    @pl.when(n == 0)
    def _():
        xb_ref[...] = x_ref[0].astype(jnp.bfloat16)

    acc = jnp.dot(xb_ref[...], w_ref[...], preferred_element_type=jnp.float32)
    acc = acc * jnp.where(n < q_tiles, scale, 1.0)
    for hh in range(heads_per_tile):
        o_ref[0, 0, hh] = acc[:, hh * dh:(hh + 1) * dh].astype(o_ref.dtype)


def _in_proj(x, w, n_heads, dh, scale, tm=1024, tn=512):
    b, s, d = x.shape
    d_inner = n_heads * dh
    tiles_per_section = d_inner // tn
    hpt = tn // dh
    kern = functools.partial(_inproj_kernel, q_tiles=tiles_per_section, scale=scale,
                             heads_per_tile=hpt, dh=dh)
    return pl.pallas_call(
        kern,
        grid=(b, s // tm, 4 * tiles_per_section),
        in_specs=[pl.BlockSpec((1, tm, d), lambda bi, i, n: (bi, i, 0)),
                  pl.BlockSpec((d, tn), lambda bi, i, n: (0, n))],
        out_specs=pl.BlockSpec(
            (1, 1, hpt, tm, dh),
            lambda bi, i, n: (n // tiles_per_section, bi, n % tiles_per_section, i, 0)),
        out_shape=jax.ShapeDtypeStruct((4, b, n_heads, s, dh), jnp.bfloat16),
        scratch_shapes=[pltpu.VMEM((tm, d), jnp.bfloat16)],
        compiler_params=_params("parallel", "parallel", "arbitrary"),
        name="in_proj",
    )(x, w)


def _gate_and_store(o, z_ref, o_ref):
    z = z_ref[0, 0, 0].astype(jnp.float32)
    silu = z * (1.0 / (1.0 + jnp.exp(-z)))
    o_ref[0] = (o * silu).astype(o_ref.dtype)


def _per_q_tile(n_tiles, body):
    i = pl.program_id(2)
    for c in range(n_tiles):
        pl.when(i == c)(functools.partial(body, c))


def _fox_kernel(q_ref, k_ref, v_ref, z_ref, ck_ref, o_ref, vaug_ref, *, t, dh, n_tiles):
    @pl.when(pl.program_id(2) == 0)
    def _():
        col = lax.broadcasted_iota(jnp.int32, (v_ref.shape[3], dh), 1)
        vaug_ref[:, :dh] = v_ref[0, 0, 0]
        vaug_ref[:, dh:] = jnp.where(col == 0, 1.0, 0.0).astype(vaug_ref.dtype)

    def body(c):
        nk = (c + 1) * t
        q = q_ref[0, 0, 0]
        s = lax.dot_general(q, k_ref[0, 0, 0, :nk, :], _NT, preferred_element_type=jnp.float32)
        ck = ck_ref[0, 0, :, :nk]
        s = s - (ck - ck[:, c * t:c * t + 1]) * LOG2E
        row = lax.broadcasted_iota(jnp.int32, (t, t), 0)
        col = lax.broadcasted_iota(jnp.int32, (t, t), 1)
        diag = jnp.where(col <= row, s[:, c * t:], NEG_BIG)
        s = diag if c == 0 else jnp.concatenate([s[:, :c * t], diag], axis=1)
        m = jnp.max(s, axis=-1, keepdims=True)
        p = jnp.exp2(s - m).astype(jnp.bfloat16)
        acc = jnp.dot(p, vaug_ref[:nk, :], preferred_element_type=jnp.float32)
        _gate_and_store(acc[:, :dh] / acc[:, dh:dh + 1], z_ref, o_ref)

    _per_q_tile(n_tiles, body)


def _sb_kernel(q_ref, k_ref, v_ref, z_ref, u_ref, o_ref, *, t, dh, n_tiles):
    ch = u_ref.shape[0]
    per_tile = t // ch

    def body(c):
        nk = (c + 1) * t
        q = q_ref[0, 0, 0]
        u = u_ref[...]
        z = lax.dot_general(q, k_ref[0, 0, 0, :nk, :], _NT, preferred_element_type=jnp.float32)
        neg_abs = lax.bitcast_convert_type(
            lax.bitcast_convert_type(z, jnp.uint32) | jnp.uint32(0x80000000), jnp.float32)
        sp = jnp.maximum(z, 0.0) + jnp.log2(1.0 + jnp.exp2(neg_abs))
        row = lax.broadcasted_iota(jnp.int32, (t, ch), 0)
        col = lax.broadcasted_iota(jnp.int32, (t, ch), 1)
        tail = jnp.zeros((t, 1), jnp.float32)
        acc = jnp.zeros((t, dh), jnp.float32)
        for b in reversed(range(nk // ch)):
            sl = slice(b * ch, (b + 1) * ch)
            sp_b, z_b = sp[:, sl], z[:, sl]
            on_diag = b >= c * per_tile
            if on_diag:
                strict = col + (b - c * per_tile) * ch < row
                sp_b = jnp.where(strict, sp_b, 0.0)
            hi = sp_b.astype(jnp.bfloat16)
            lo = (sp_b - hi.astype(jnp.float32)).astype(jnp.bfloat16)
            r = (jnp.dot(hi, u, preferred_element_type=jnp.float32)
                 + jnp.dot(lo, u, preferred_element_type=jnp.float32))
            w = jnp.exp2(z_b - r - tail)
            if on_diag:
                w = jnp.where(strict, w, 0.0)
            acc = acc + jnp.dot(w.astype(jnp.bfloat16), v_ref[0, 0, 0, sl, :],
                                preferred_element_type=jnp.float32)
            tail = tail + r[:, 0:1]
        _gate_and_store(acc, z_ref, o_ref)

    _per_q_tile(n_tiles, body)


def _mixer(qkvz, extra, *, fox, t):
    _, b, h, s, dh = qkvz.shape
    n_tiles = s // t
    tile = lambda sec: pl.BlockSpec((1, 1, 1, t, dh), lambda bi, hi, i: (sec, bi, hi, i, 0))
    full = lambda sec: pl.BlockSpec((1, 1, 1, s, dh), lambda bi, hi, i: (sec, bi, hi, 0, 0))
    if fox:
        kern = functools.partial(_fox_kernel, t=t, dh=dh, n_tiles=n_tiles)
        extra_spec = pl.BlockSpec((1, 1, 1, s), lambda bi, hi, i: (bi, hi, 0, 0))
        scratch = [pltpu.VMEM((s, 2 * dh), jnp.bfloat16)]
    else:
        kern = functools.partial(_sb_kernel, t=t, dh=dh, n_tiles=n_tiles)
        extra_spec = pl.BlockSpec(extra.shape, lambda bi, hi, i: (0, 0))
        scratch = []
    return pl.pallas_call(
        kern,
        grid=(b, h, n_tiles),
        in_specs=[tile(0), full(1), full(2), tile(3), extra_spec],
        out_specs=pl.BlockSpec((1, t, dh), lambda bi, hi, i: (bi, i, hi)),
        out_shape=jax.ShapeDtypeStruct((b, s, h * dh), jnp.bfloat16),
        scratch_shapes=scratch,
        compiler_params=_params("parallel", "parallel", "arbitrary"),
        name="fox_mixer" if fox else "sb_mixer",
    )(qkvz, qkvz, qkvz, qkvz, extra)


def _outproj_kernel(g_ref, w_ref, x_ref, lg_ref, lb_ref, o_ref, *, alpha):
    y = jnp.dot(g_ref[0], w_ref[...], preferred_element_type=jnp.float32)
    r = alpha * x_ref[0] + y
    mu = jnp.mean(r, axis=-1, keepdims=True)
    d = r - mu
    var = jnp.mean(d * d, axis=-1, keepdims=True)
    o_ref[0] = d * lax.rsqrt(var + LN_EPS) * lg_ref[...] + lb_ref[...]


def _out_proj_norm(g, w, x, ln_g, ln_b, alpha, tm=512):
    b, s, d = x.shape
    d_inner = g.shape[-1]
    return pl.pallas_call(
        functools.partial(_outproj_kernel, alpha=alpha),
        grid=(b, s // tm),
        in_specs=[pl.BlockSpec((1, tm, d_inner), lambda bi, i: (bi, i, 0)),
                  pl.BlockSpec((d_inner, d), lambda bi, i: (0, 0)),
                  pl.BlockSpec((1, tm, d), lambda bi, i: (bi, i, 0)),
                  pl.BlockSpec((1, d), lambda bi, i: (0, 0)),
                  pl.BlockSpec((1, d), lambda bi, i: (0, 0))],
        out_specs=pl.BlockSpec((1, tm, d), lambda bi, i: (bi, i, 0)),
        out_shape=jax.ShapeDtypeStruct((b, s, d), jnp.float32),
        compiler_params=_params("parallel", "parallel"),
        name="out_proj_norm",
    )(g, w, x, ln_g, ln_b)


def kernel(x, fox_w_in, fox_b_f, fox_w_out, sb_w_in, sb_w_out, ln_g, ln_b):
    b, s, d = x.shape
    depth = ln_g.shape[0]
    n_heads = fox_b_f.shape[1]
    d_inner = fox_w_out.shape[1]
    dh = d_inner // n_heads
    scale = dh ** -0.5
    alpha = (2 * depth) ** 0.25
    t = min(512, s)
    ch = min(MXU_DIM, t)

    u = (jnp.arange(ch)[:, None] >= jnp.arange(ch)[None, :]).astype(jnp.bfloat16)
    for layer in range(depth):
        slot = layer // 2
        fox = layer % 2 == 0
        w_in = fox_w_in[slot] if fox else sb_w_in[slot]
        w_out = fox_w_out[slot] if fox else sb_w_out[slot]
        qkvz = _in_proj(x, w_in[:, :4 * d_inner].astype(jnp.bfloat16), n_heads, dh,
                        scale * LOG2E, tm=min(1024, s), tn=min(512, d_inner))
        if fox:
            wf = jnp.pad(w_in[:, 4 * d_inner:], ((0, 0), (0, LANES - n_heads)))
            bf = jnp.pad(fox_b_f[slot], (0, LANES - n_heads))[None, :]
            c = _fox_gates(x, wf, bf, ts=min(512, s))
            ck = jnp.transpose(c[:, :, :n_heads], (0, 2, 1))[:, :, None, :]
            g = _mixer(qkvz, ck, fox=True, t=t)
        else:
            g = _mixer(qkvz, u, fox=False, t=t)
        x = _out_proj_norm(g, w_out.astype(jnp.bfloat16), x, ln_g[layer][None, :],
                           ln_b[layer][None, :], alpha, tm=min(512, s))
    return x
```

```python
import functools
import math

import jax
import jax.numpy as jnp
from jax import lax
from jax.experimental import pallas as pl
from jax.experimental.pallas import tpu as pltpu

LANES = 128
MXU_DIM = 256
LN_EPS = 1e-5
NEG_BIG = -1e30
LOG2E = math.log2(math.e)
VMEM_LIMIT = 56 * 1024 * 1024

_NT = (((1,), (1,)), ((), ()))


def _params(*sem):
    return pltpu.CompilerParams(dimension_semantics=sem, vmem_limit_bytes=VMEM_LIMIT)


def _gates_kernel(x_ref, wf_ref, bf_ref, tri_ref, c_ref, carry_ref):
    @pl.when(pl.program_id(1) == 0)
    def _():
        carry_ref[...] = jnp.zeros_like(carry_ref)

    f = jnp.dot(x_ref[0], wf_ref[...], preferred_element_type=jnp.float32,
                precision=lax.Precision.HIGHEST) + bf_ref[...]
    log_f = jnp.minimum(f, 0.0) - jnp.log(1.0 + jnp.exp(-jnp.abs(f)))
    c = jnp.dot(tri_ref[...], log_f, preferred_element_type=jnp.float32,
                precision=lax.Precision.HIGHEST) + carry_ref[...]
    c_ref[0] = c
    carry_ref[...] = c[-1:, :]


def _fox_gates(x, wf, bf, ts=512):
    b, s, d = x.shape
    tri = (jnp.arange(ts)[:, None] >= jnp.arange(ts)[None, :]).astype(jnp.float32)
    return pl.pallas_call(
        _gates_kernel,
        grid=(b, s // ts),
        in_specs=[pl.BlockSpec((1, ts, d), lambda i, j: (i, j, 0)),
                  pl.BlockSpec((d, LANES), lambda i, j: (0, 0)),
                  pl.BlockSpec((1, LANES), lambda i, j: (0, 0)),
                  pl.BlockSpec((ts, ts), lambda i, j: (0, 0))],
        out_specs=pl.BlockSpec((1, ts, LANES), lambda i, j: (i, j, 0)),
        out_shape=jax.ShapeDtypeStruct((b, s, LANES), jnp.float32),
        scratch_shapes=[pltpu.VMEM((1, LANES), jnp.float32)],
        compiler_params=_params("parallel", "arbitrary"),
        name="fox_gates",
    )(x, wf, bf, tri)


def _inproj_kernel(x_ref, w_ref, o_ref, xb_ref, *, q_tiles, scale, heads_per_tile, dh):
    n = pl.program_id(2)

    @pl.when(n == 0)
    def _():
        xb_ref[...] = x_ref[0].astype(jnp.bfloat16)

    acc = jnp.dot(xb_ref[...], w_ref[...], preferred_element_type=jnp.float32)
    acc = acc * jnp.where(n < q_tiles, scale, 1.0)
    for hh in range(heads_per_tile):
        o_ref[0, 0, hh] = acc[:, hh * dh:(hh + 1) * dh].astype(o_ref.dtype)


def _in_proj(x, w, n_heads, dh, scale, tm=1024, tn=512):
    b, s, d = x.shape
    d_inner = n_heads * dh
    tiles_per_section = d_inner // tn
    hpt = tn // dh
    kern = functools.partial(_inproj_kernel, q_tiles=tiles_per_section, scale=scale,
                             heads_per_tile=hpt, dh=dh)
    return pl.pallas_call(
        kern,
        grid=(b, s // tm, 4 * tiles_per_section),
        in_specs=[pl.BlockSpec((1, tm, d), lambda bi, i, n: (bi, i, 0)),
                  pl.BlockSpec((d, tn), lambda bi, i, n: (0, n))],
        out_specs=pl.BlockSpec(
            (1, 1, hpt, tm, dh),
            lambda bi, i, n: (n // tiles_per_section, bi, n % tiles_per_section, i, 0)),
        out_shape=jax.ShapeDtypeStruct((4, b, n_heads, s, dh), jnp.bfloat16),
        scratch_shapes=[pltpu.VMEM((tm, d), jnp.bfloat16)],
        compiler_params=_params("parallel", "parallel", "arbitrary"),
        name="in_proj",
    )(x, w)


def _gate_and_store(o, z_ref, o_ref, rows):
    z = z_ref[0, 0, 0, rows, :].astype(jnp.float32)
    silu = z * (1.0 / (1.0 + jnp.exp(-z)))
    o_ref[0, rows, :] = (o * silu).astype(o_ref.dtype)


def _fox_kernel(q_ref, k_ref, v_ref, z_ref, ck_ref, o_ref, vaug_ref, *, t, dh, n_tiles):
    col = lax.broadcasted_iota(jnp.int32, (v_ref.shape[3], dh), 1)
    vaug_ref[:, :dh] = v_ref[0, 0, 0]
    vaug_ref[:, dh:] = jnp.where(col == 0, 1.0, 0.0).astype(vaug_ref.dtype)

    def body(c):
        rows = slice(c * t, (c + 1) * t)
        q = q_ref[0, 0, 0, rows, :]
        c_tile = ck_ref[0, 0, :, c * t:c * t + 1]
        m = acc = None
        for j in range(c + 1):
            sl = slice(j * t, (j + 1) * t)
            s = lax.dot_general(q, k_ref[0, 0, 0, sl, :], _NT, preferred_element_type=jnp.float32)
            s = s - (ck_ref[0, 0, :, sl] - c_tile) * LOG2E
            if j == c:
                row = lax.broadcasted_iota(jnp.int32, (t, t), 0)
                col = lax.broadcasted_iota(jnp.int32, (t, t), 1)
                s = jnp.where(col <= row, s, NEG_BIG)
            bm = jnp.max(s, axis=-1, keepdims=True)
            m_new = bm if j == 0 else jnp.maximum(m, bm)
            p = jnp.exp2(s - m_new).astype(jnp.bfloat16)
            pv = jnp.dot(p, vaug_ref[sl, :], preferred_element_type=jnp.float32)
            acc = pv if j == 0 else acc * jnp.exp2(m - m_new) + pv
            m = m_new
        _gate_and_store(acc[:, :dh] / acc[:, dh:dh + 1], z_ref, o_ref, rows)

    for c in range(n_tiles):
        body(c)


def _sb_kernel(q_ref, k_ref, v_ref, z_ref, u_ref, o_ref, *, t, dh, n_tiles):
    ch = u_ref.shape[0]
    per_tile = t // ch

    def body(c):
        nk = (c + 1) * t
        rows = slice(c * t, (c + 1) * t)
        q = q_ref[0, 0, 0, rows, :]
        u = u_ref[...]
        z = lax.dot_general(q, k_ref[0, 0, 0, :nk, :], _NT, preferred_element_type=jnp.float32)
        neg_abs = lax.bitcast_convert_type(
            lax.bitcast_convert_type(z, jnp.uint32) | jnp.uint32(0x80000000), jnp.float32)
        sp = jnp.maximum(z, 0.0) + jnp.log2(1.0 + jnp.exp2(neg_abs))
        row = lax.broadcasted_iota(jnp.int32, (t, ch), 0)
        col = lax.broadcasted_iota(jnp.int32, (t, ch), 1)
        tail = jnp.zeros((t, 1), jnp.float32)
        acc = jnp.zeros((t, dh), jnp.float32)
        for b in reversed(range(nk // ch)):
            sl = slice(b * ch, (b + 1) * ch)
            sp_b, z_b = sp[:, sl], z[:, sl]
            on_diag = b >= c * per_tile
            if on_diag:
                strict = col + (b - c * per_tile) * ch < row
                sp_b = jnp.where(strict, sp_b, 0.0)
            hi = sp_b.astype(jnp.bfloat16)
            lo = (sp_b - hi.astype(jnp.float32)).astype(jnp.bfloat16)
            r = (jnp.dot(hi, u, preferred_element_type=jnp.float32)
                 + jnp.dot(lo, u, preferred_element_type=jnp.float32))
            w = jnp.exp2(z_b - r - tail)
            if on_diag:
                w = jnp.where(strict, w, 0.0)
            acc = acc + jnp.dot(w.astype(jnp.bfloat16), v_ref[0, 0, 0, sl, :],
                                preferred_element_type=jnp.float32)
            tail = tail + r[:, 0:1]
        _gate_and_store(acc, z_ref, o_ref, rows)

    for c in range(n_tiles):
        body(c)


def _mixer(qkvz, extra, *, fox, t):
    _, b, h, s, dh = qkvz.shape
    n_tiles = s // t
    full = lambda sec: pl.BlockSpec((1, 1, 1, s, dh), lambda bi, hi: (sec, bi, hi, 0, 0))
    if fox:
        kern = functools.partial(_fox_kernel, t=t, dh=dh, n_tiles=n_tiles)
        extra_spec = pl.BlockSpec((1, 1, 1, s), lambda bi, hi: (bi, hi, 0, 0))
        scratch = [pltpu.VMEM((s, 2 * dh), jnp.bfloat16)]
    else:
        kern = functools.partial(_sb_kernel, t=t, dh=dh, n_tiles=n_tiles)
        extra_spec = pl.BlockSpec(extra.shape, lambda bi, hi: (0, 0))
        scratch = []
    return pl.pallas_call(
        kern,
        grid=(b, h),
        in_specs=[full(0), full(1), full(2), full(3), extra_spec],
        out_specs=pl.BlockSpec((1, s, dh), lambda bi, hi: (bi, 0, hi)),
        out_shape=jax.ShapeDtypeStruct((b, s, h * dh), jnp.bfloat16),
        scratch_shapes=scratch,
        compiler_params=_params("parallel", "parallel"),
        name="fox_mixer" if fox else "sb_mixer",
    )(qkvz, qkvz, qkvz, qkvz, extra)


def _outproj_kernel(g_ref, w_ref, x_ref, lg_ref, lb_ref, o_ref, *, alpha):
    y = jnp.dot(g_ref[0], w_ref[...], preferred_element_type=jnp.float32)
    r = alpha * x_ref[0] + y
    mu = jnp.mean(r, axis=-1, keepdims=True)
    d = r - mu
    var = jnp.mean(d * d, axis=-1, keepdims=True)
    o_ref[0] = d * lax.rsqrt(var + LN_EPS) * lg_ref[...] + lb_ref[...]


def _out_proj_norm(g, w, x, ln_g, ln_b, alpha, tm=512):
    b, s, d = x.shape
    d_inner = g.shape[-1]
    return pl.pallas_call(
        functools.partial(_outproj_kernel, alpha=alpha),
        grid=(b, s // tm),
        in_specs=[pl.BlockSpec((1, tm, d_inner), lambda bi, i: (bi, i, 0)),
                  pl.BlockSpec((d_inner, d), lambda bi, i: (0, 0)),
                  pl.BlockSpec((1, tm, d), lambda bi, i: (bi, i, 0)),
                  pl.BlockSpec((1, d), lambda bi, i: (0, 0)),
                  pl.BlockSpec((1, d), lambda bi, i: (0, 0))],
        out_specs=pl.BlockSpec((1, tm, d), lambda bi, i: (bi, i, 0)),
        out_shape=jax.ShapeDtypeStruct((b, s, d), jnp.float32),
        compiler_params=_params("parallel", "parallel"),
        name="out_proj_norm",
    )(g, w, x, ln_g, ln_b)


def kernel(x, fox_w_in, fox_b_f, fox_w_out, sb_w_in, sb_w_out, ln_g, ln_b):
    b, s, d = x.shape
    depth = ln_g.shape[0]
    n_heads = fox_b_f.shape[1]
    d_inner = fox_w_out.shape[1]
    dh = d_inner // n_heads
    scale = dh ** -0.5
    alpha = (2 * depth) ** 0.25
    t = min(512, s)
    ch = min(MXU_DIM, t)

    u = (jnp.arange(ch)[:, None] >= jnp.arange(ch)[None, :]).astype(jnp.bfloat16)
    for layer in range(depth):
        slot = layer // 2
        fox = layer % 2 == 0
        w_in = fox_w_in[slot] if fox else sb_w_in[slot]
        w_out = fox_w_out[slot] if fox else sb_w_out[slot]
        qkvz = _in_proj(x, w_in[:, :4 * d_inner].astype(jnp.bfloat16), n_heads, dh,
                        scale * LOG2E, tm=min(1024, s), tn=min(512, d_inner))
        if fox:
            wf = jnp.pad(w_in[:, 4 * d_inner:], ((0, 0), (0, LANES - n_heads)))
            bf = jnp.pad(fox_b_f[slot], (0, LANES - n_heads))[None, :]
            c = _fox_gates(x, wf, bf, ts=min(512, s))
            ck = jnp.transpose(c[:, :, :n_heads], (0, 2, 1))[:, :, None, :]
            g = _mixer(qkvz, ck, fox=True, t=t)
        else:
            g = _mixer(qkvz, u, fox=False, t=t)
        x = _out_proj_norm(g, w_out.astype(jnp.bfloat16), x, ln_g[layer][None, :],
                           ln_b[layer][None, :], alpha, tm=min(512, s))
    return x
```

```python
import functools
import math

import jax
import jax.numpy as jnp
from jax import lax
from jax.experimental import pallas as pl
from jax.experimental.pallas import tpu as pltpu

LANES = 128
MXU_DIM = 256
LN_EPS = 1e-5
NEG_BIG = -1e30
LOG2E = math.log2(math.e)
VMEM_LIMIT = 56 * 1024 * 1024

_NT = (((1,), (1,)), ((), ()))


def _params(*sem):
    return pltpu.CompilerParams(dimension_semantics=sem, vmem_limit_bytes=VMEM_LIMIT)


def _gates_kernel(x_ref, wf_ref, bf_ref, tri_ref, c_ref, carry_ref):
    @pl.when(pl.program_id(1) == 0)
    def _():
        carry_ref[...] = jnp.zeros_like(carry_ref)

    f = jnp.dot(x_ref[0], wf_ref[...], preferred_element_type=jnp.float32,
                precision=lax.Precision.HIGHEST) + bf_ref[...]
    log_f = jnp.minimum(f, 0.0) - jnp.log(1.0 + jnp.exp(-jnp.abs(f)))
    c = jnp.dot(tri_ref[...], log_f, preferred_element_type=jnp.float32,
                precision=lax.Precision.HIGHEST) + carry_ref[...]
    c_ref[0] = c
    carry_ref[...] = c[-1:, :]


def _fox_gates(x, wf, bf, ts=512):
    b, s, d = x.shape
    tri = (jnp.arange(ts)[:, None] >= jnp.arange(ts)[None, :]).astype(jnp.float32)
    return pl.pallas_call(
        _gates_kernel,
        grid=(b, s // ts),
        in_specs=[pl.BlockSpec((1, ts, d), lambda i, j: (i, j, 0)),
                  pl.BlockSpec((d, LANES), lambda i, j: (0, 0)),
                  pl.BlockSpec((1, LANES), lambda i, j: (0, 0)),
                  pl.BlockSpec((ts, ts), lambda i, j: (0, 0))],
        out_specs=pl.BlockSpec((1, ts, LANES), lambda i, j: (i, j, 0)),
        out_shape=jax.ShapeDtypeStruct((b, s, LANES), jnp.float32),
        scratch_shapes=[pltpu.VMEM((1, LANES), jnp.float32)],
        compiler_params=_params("parallel", "arbitrary"),
        name="fox_gates",
    )(x, wf, bf, tri)


def _inproj_kernel(x_ref, w_ref, o_ref, xb_ref, *, q_tiles, scale, heads_per_tile, dh):
    n = pl.program_id(2)

    @pl.when(n == 0)
    def _():
        xb_ref[...] = x_ref[0].astype(jnp.bfloat16)

    acc = jnp.dot(xb_ref[...], w_ref[...], preferred_element_type=jnp.float32)
    acc = acc * jnp.where(n < q_tiles, scale, 1.0)
    for hh in range(heads_per_tile):
        o_ref[0, 0, hh] = acc[:, hh * dh:(hh + 1) * dh].astype(o_ref.dtype)


def _in_proj(x, w, n_heads, dh, scale, tm=1024, tn=512):
    b, s, d = x.shape
    d_inner = n_heads * dh
    tiles_per_section = d_inner // tn
    hpt = tn // dh
    kern = functools.partial(_inproj_kernel, q_tiles=tiles_per_section, scale=scale,
                             heads_per_tile=hpt, dh=dh)
    return pl.pallas_call(
        kern,
        grid=(b, s // tm, 4 * tiles_per_section),
        in_specs=[pl.BlockSpec((1, tm, d), lambda bi, i, n: (bi, i, 0)),
                  pl.BlockSpec((d, tn), lambda bi, i, n: (0, n))],
        out_specs=pl.BlockSpec(
            (1, 1, hpt, tm, dh),
            lambda bi, i, n: (n // tiles_per_section, bi, n % tiles_per_section, i, 0)),
        out_shape=jax.ShapeDtypeStruct((4, b, n_heads, s, dh), jnp.bfloat16),
        scratch_shapes=[pltpu.VMEM((tm, d), jnp.bfloat16)],
        compiler_params=_params("parallel", "parallel", "arbitrary"),
        name="in_proj",
    )(x, w)


def _gate_and_store(o, z_ref, o_ref, rows):
    z = z_ref[0, 0, 0, rows, :].astype(jnp.float32)
    silu = z * (1.0 / (1.0 + jnp.exp(-z)))
    o_ref[0, rows, :] = (o * silu).astype(o_ref.dtype)


def _fox_kernel(q_ref, k_ref, v_ref, z_ref, ck_ref, o_ref, vaug_ref, *, t, dh, n_tiles):
    col = lax.broadcasted_iota(jnp.int32, (v_ref.shape[3], dh), 1)
    vaug_ref[:, :dh] = v_ref[0, 0, 0]
    vaug_ref[:, dh:] = jnp.where(col == 0, 1.0, 0.0).astype(vaug_ref.dtype)

    def body(c):
        rows = slice(c * t, (c + 1) * t)
        q = q_ref[0, 0, 0, rows, :]
        c_tile = ck_ref[0, 0, :, c * t:c * t + 1]
        m = acc = None
        for j in range(c + 1):
            sl = slice(j * t, (j + 1) * t)
            s = lax.dot_general(q, k_ref[0, 0, 0, sl, :], _NT, preferred_element_type=jnp.float32)
            s = s - (ck_ref[0, 0, :, sl] - c_tile) * LOG2E
            if j == c:
                row = lax.broadcasted_iota(jnp.int32, (t, t), 0)
                col = lax.broadcasted_iota(jnp.int32, (t, t), 1)
                s = jnp.where(col <= row, s, NEG_BIG)
            bm = jnp.max(s, axis=-1, keepdims=True)
            m_new = bm if j == 0 else jnp.maximum(m, bm)
            p = jnp.exp2(s - m_new).astype(jnp.bfloat16)
            pv = jnp.dot(p, vaug_ref[sl, :], preferred_element_type=jnp.float32)
            acc = pv if j == 0 else acc * jnp.exp2(m - m_new) + pv
            m = m_new
        _gate_and_store(acc[:, :dh] / acc[:, dh:dh + 1], z_ref, o_ref, rows)

    for c in range(n_tiles):
        body(c)


def _sb_kernel(q_ref, k_ref, v_ref, z_ref, u_ref, o_ref, *, t, dh, n_tiles):
    ch = u_ref.shape[0]
    per_tile = t // ch

    def body(c):
        nk = (c + 1) * t
        rows = slice(c * t, (c + 1) * t)
        q = q_ref[0, 0, 0, rows, :]
        u = u_ref[...]
        z = lax.dot_general(q, k_ref[0, 0, 0, :nk, :], _NT, preferred_element_type=jnp.float32)
        neg_abs = lax.bitcast_convert_type(
            lax.bitcast_convert_type(z, jnp.uint32) | jnp.uint32(0x80000000), jnp.float32)
        sp = jnp.maximum(z, 0.0) + jnp.log2(1.0 + jnp.exp2(neg_abs))
        row = lax.broadcasted_iota(jnp.int32, (t, ch), 0)
        col = lax.broadcasted_iota(jnp.int32, (t, ch), 1)
        tail = jnp.zeros((t, 1), jnp.float32)
        acc = jnp.zeros((t, dh), jnp.float32)
        for b in reversed(range(nk // ch)):
            sl = slice(b * ch, (b + 1) * ch)
            sp_b, z_b = sp[:, sl], z[:, sl]
            on_diag = b >= c * per_tile
            if on_diag:
                strict = col + (b - c * per_tile) * ch < row
                sp_b = jnp.where(strict, sp_b, 0.0)
            r = jnp.dot(sp_b.astype(jnp.bfloat16), u, preferred_element_type=jnp.float32)
            w = jnp.exp2(z_b - r - tail)
            if on_diag:
                w = jnp.where(strict, w, 0.0)
            acc = acc + jnp.dot(w.astype(jnp.bfloat16), v_ref[0, 0, 0, sl, :],
                                preferred_element_type=jnp.float32)
            tail = tail + r[:, 0:1]
        _gate_and_store(acc, z_ref, o_ref, rows)

    for c in range(n_tiles):
        body(c)


def _mixer(qkvz, extra, *, fox, t):
    _, b, h, s, dh = qkvz.shape
    n_tiles = s // t
    full = lambda sec: pl.BlockSpec((1, 1, 1, s, dh), lambda bi, hi: (sec, bi, hi, 0, 0))
    if fox:
        kern = functools.partial(_fox_kernel, t=t, dh=dh, n_tiles=n_tiles)
        extra_spec = pl.BlockSpec((1, 1, 1, s), lambda bi, hi: (bi, hi, 0, 0))
        scratch = [pltpu.VMEM((s, 2 * dh), jnp.bfloat16)]
    else:
        kern = functools.partial(_sb_kernel, t=t, dh=dh, n_tiles=n_tiles)
        extra_spec = pl.BlockSpec(extra.shape, lambda bi, hi: (0, 0))
        scratch = []
    return pl.pallas_call(
        kern,
        grid=(b, h),
        in_specs=[full(0), full(1), full(2), full(3), extra_spec],
        out_specs=pl.BlockSpec((1, s, dh), lambda bi, hi: (bi, 0, hi)),
        out_shape=jax.ShapeDtypeStruct((b, s, h * dh), jnp.bfloat16),
        scratch_shapes=scratch,
        compiler_params=_params("parallel", "parallel"),
        name="fox_mixer" if fox else "sb_mixer",
    )(qkvz, qkvz, qkvz, qkvz, extra)


def _outproj_kernel(g_ref, w_ref, x_ref, lg_ref, lb_ref, o_ref, *, alpha):
    y = jnp.dot(g_ref[0], w_ref[...], preferred_element_type=jnp.float32)
    r = alpha * x_ref[0] + y
    mu = jnp.mean(r, axis=-1, keepdims=True)
    d = r - mu
    var = jnp.mean(d * d, axis=-1, keepdims=True)
    o_ref[0] = d * lax.rsqrt(var + LN_EPS) * lg_ref[...] + lb_ref[...]


def _out_proj_norm(g, w, x, ln_g, ln_b, alpha, tm=512):
    b, s, d = x.shape
    d_inner = g.shape[-1]
    return pl.pallas_call(
        functools.partial(_outproj_kernel, alpha=alpha),
        grid=(b, s // tm),
        in_specs=[pl.BlockSpec((1, tm, d_inner), lambda bi, i: (bi, i, 0)),
                  pl.BlockSpec((d_inner, d), lambda bi, i: (0, 0)),
                  pl.BlockSpec((1, tm, d), lambda bi, i: (bi, i, 0)),
                  pl.BlockSpec((1, d), lambda bi, i: (0, 0)),
                  pl.BlockSpec((1, d), lambda bi, i: (0, 0))],
        out_specs=pl.BlockSpec((1, tm, d), lambda bi, i: (bi, i, 0)),
        out_shape=jax.ShapeDtypeStruct((b, s, d), jnp.float32),
        compiler_params=_params("parallel", "parallel"),
        name="out_proj_norm",
    )(g, w, x, ln_g, ln_b)


def kernel(x, fox_w_in, fox_b_f, fox_w_out, sb_w_in, sb_w_out, ln_g, ln_b):
    b, s, d = x.shape
    depth = ln_g.shape[0]
    n_heads = fox_b_f.shape[1]
    d_inner = fox_w_out.shape[1]
    dh = d_inner // n_heads
    scale = dh ** -0.5
    alpha = (2 * depth) ** 0.25
    t = min(512, s)
    ch = min(MXU_DIM, t)

    u = (jnp.arange(ch)[:, None] >= jnp.arange(ch)[None, :]).astype(jnp.bfloat16)
    for layer in range(depth):
        slot = layer // 2
        fox = layer % 2 == 0
        w_in = fox_w_in[slot] if fox else sb_w_in[slot]
        w_out = fox_w_out[slot] if fox else sb_w_out[slot]
        qkvz = _in_proj(x, w_in[:, :4 * d_inner].astype(jnp.bfloat16), n_heads, dh,
                        scale * LOG2E, tm=min(1024, s), tn=min(1024, d_inner))
        if fox:
            wf = jnp.pad(w_in[:, 4 * d_inner:], ((0, 0), (0, LANES - n_heads)))
            bf = jnp.pad(fox_b_f[slot], (0, LANES - n_heads))[None, :]
            c = _fox_gates(x, wf, bf, ts=min(512, s))
            ck = jnp.transpose(c[:, :, :n_heads], (0, 2, 1))[:, :, None, :]
            g = _mixer(qkvz, ck, fox=True, t=t)
        else:
            g = _mixer(qkvz, u, fox=False, t=t)
        x = _out_proj_norm(g, w_out.astype(jnp.bfloat16), x, ln_g[layer][None, :],
                           ln_b[layer][None, :], alpha, tm=min(512, s))
    return x
```

```python
import functools
import math

import jax
import jax.numpy as jnp
from jax import lax
from jax.experimental import pallas as pl
from jax.experimental.pallas import tpu as pltpu

LANES = 128
MXU_DIM = 256
LN_EPS = 1e-5
NEG_BIG = -1e30
NEGLIGIBLE_BITS = 192.0
LOG2E = math.log2(math.e)
VMEM_LIMIT = 56 * 1024 * 1024

_NT = (((1,), (1,)), ((), ()))


def _params(*sem):
    return pltpu.CompilerParams(dimension_semantics=sem, vmem_limit_bytes=VMEM_LIMIT)


def _split_bf16(a, terms):
    out = []
    for _ in range(terms):
        piece = a.astype(jnp.bfloat16)
        out.append(piece)
        a = a - piece.astype(jnp.float32)
    return out


def _gates_kernel(x_ref, wh_ref, wl_ref, bf_ref, tri_ref, c_ref, carry_ref):
    @pl.when(pl.program_id(1) == 0)
    def _():
        carry_ref[...] = jnp.zeros_like(carry_ref)

    dot = functools.partial(jnp.dot, preferred_element_type=jnp.float32)
    xh, xl = _split_bf16(x_ref[0], 2)
    f = dot(xh, wh_ref[...]) + dot(xh, wl_ref[...]) + dot(xl, wh_ref[...]) + bf_ref[...]
    log_f = jnp.minimum(f, 0.0) - jnp.log(1.0 + jnp.exp(-jnp.abs(f)))
    c = carry_ref[...]
    for piece in _split_bf16(log_f, 3):
        c = c + dot(tri_ref[...], piece)
    c_ref[0] = c
    carry_ref[...] = c[-1:, :]


def _fox_gates(x, wf, bf, ts=512):
    b, s, d = x.shape
    tri = (jnp.arange(ts)[:, None] >= jnp.arange(ts)[None, :]).astype(jnp.bfloat16)
    wh, wl = _split_bf16(wf, 2)
    return pl.pallas_call(
        _gates_kernel,
        grid=(b, s // ts),
        in_specs=[pl.BlockSpec((1, ts, d), lambda i, j: (i, j, 0)),
                  pl.BlockSpec((d, LANES), lambda i, j: (0, 0)),
                  pl.BlockSpec((d, LANES), lambda i, j: (0, 0)),
                  pl.BlockSpec((1, LANES), lambda i, j: (0, 0)),
                  pl.BlockSpec((ts, ts), lambda i, j: (0, 0))],
        out_specs=pl.BlockSpec((1, ts, LANES), lambda i, j: (i, j, 0)),
        out_shape=jax.ShapeDtypeStruct((b, s, LANES), jnp.float32),
        scratch_shapes=[pltpu.VMEM((1, LANES), jnp.float32)],
        compiler_params=_params("parallel", "arbitrary"),
        name="fox_gates",
    )(x, wh, wl, bf, tri)


def _inproj_kernel(x_ref, w_ref, o_ref, xb_ref, *, q_tiles, scale, heads_per_tile, dh):
    n = pl.program_id(2)

    @pl.when(n == 0)
    def _():
        xb_ref[...] = x_ref[0].astype(jnp.bfloat16)

    acc = jnp.dot(xb_ref[...], w_ref[...], preferred_element_type=jnp.float32)
    acc = acc * jnp.where(n < q_tiles, scale, 1.0)
    for hh in range(heads_per_tile):
        o_ref[0, 0, hh] = acc[:, hh * dh:(hh + 1) * dh].astype(o_ref.dtype)


def _in_proj(x, w, n_heads, dh, scale, tm=1024, tn=512):
    b, s, d = x.shape
    d_inner = n_heads * dh
    tiles_per_section = d_inner // tn
    hpt = tn // dh
    kern = functools.partial(_inproj_kernel, q_tiles=tiles_per_section, scale=scale,
                             heads_per_tile=hpt, dh=dh)
    return pl.pallas_call(
        kern,
        grid=(b, s // tm, 4 * tiles_per_section),
        in_specs=[pl.BlockSpec((1, tm, d), lambda bi, i, n: (bi, i, 0)),
                  pl.BlockSpec((d, tn), lambda bi, i, n: (0, n))],
        out_specs=pl.BlockSpec(
            (1, 1, hpt, tm, dh),
            lambda bi, i, n: (n // tiles_per_section, bi, n % tiles_per_section, i, 0)),
        out_shape=jax.ShapeDtypeStruct((4, b, n_heads, s, dh), jnp.bfloat16),
        scratch_shapes=[pltpu.VMEM((tm, d), jnp.bfloat16)],
        compiler_params=_params("parallel", "parallel", "arbitrary"),
        name="in_proj",
    )(x, w)


def _gate_and_store(o, z_ref, o_ref, rows):
    z = z_ref[0, 0, 0, rows, :].astype(jnp.float32)
    silu = z * (1.0 / (1.0 + jnp.exp(-z)))
    o_ref[0, rows, :] = (o * silu).astype(o_ref.dtype)


def _fox_kernel(q_ref, k_ref, v_ref, z_ref, ck_ref, o_ref, vaug_ref, *, t, dh, n_tiles):
    col = lax.broadcasted_iota(jnp.int32, (v_ref.shape[3], dh), 1)
    vaug_ref[:, :dh] = v_ref[0, 0, 0]
    vaug_ref[:, dh:] = jnp.where(col == 0, 1.0, 0.0).astype(vaug_ref.dtype)

    def body(c):
        rows = slice(c * t, (c + 1) * t)
        q = q_ref[0, 0, 0, rows, :]
        c_tile = ck_ref[0, 0, :, c * t:c * t + 1]
        m = acc = None
        for j in range(c + 1):
            sl = slice(j * t, (j + 1) * t)
            s = lax.dot_general(q, k_ref[0, 0, 0, sl, :], _NT, preferred_element_type=jnp.float32)
            s = s - (ck_ref[0, 0, :, sl] - c_tile) * LOG2E
            if j == c:
                row = lax.broadcasted_iota(jnp.int32, (t, t), 0)
                col = lax.broadcasted_iota(jnp.int32, (t, t), 1)
                s = jnp.where(col <= row, s, NEG_BIG)
            bm = jnp.max(s, axis=-1, keepdims=True)
            m_new = bm if j == 0 else jnp.maximum(m, bm)
            p = jnp.exp2(s - m_new).astype(jnp.bfloat16)
            pv = jnp.dot(p, vaug_ref[sl, :], preferred_element_type=jnp.float32)
            acc = pv if j == 0 else acc * jnp.exp2(m - m_new) + pv
            m = m_new
        _gate_and_store(acc[:, :dh] / acc[:, dh:dh + 1], z_ref, o_ref, rows)

    for c in range(n_tiles):
        body(c)


def _sb_kernel(q_ref, k_ref, v_ref, z_ref, u_ref, o_ref, acc_ref, tail_ref, *, t, dh, n_tiles):
    ch = u_ref.shape[0]
    assert t == 2 * ch
    u = u_ref[...]

    def scores(q, keys):
        z = lax.dot_general(q, k_ref[0, 0, 0, keys, :], _NT, preferred_element_type=jnp.float32)
        neg_abs = lax.bitcast_convert_type(
            lax.bitcast_convert_type(z, jnp.uint32) | jnp.uint32(0x80000000), jnp.float32)
        return z, jnp.maximum(z, 0.0) + jnp.log2(1.0 + jnp.exp2(neg_abs))

    def chunk(z_b, sp_b, keys, tail, strict):
        if strict is not None:
            sp_b = jnp.where(strict, sp_b, 0.0)
        r = jnp.dot(sp_b.astype(jnp.bfloat16), u, preferred_element_type=jnp.float32)
        e = z_b - r if tail is None else z_b - r - tail
        w = jnp.exp2(e)
        if strict is not None:
            w = jnp.where(strict, w, 0.0)
        pv = jnp.dot(w.astype(jnp.bfloat16), v_ref[0, 0, 0, keys, :],
                     preferred_element_type=jnp.float32)
        return pv, r[:, 0:1]

    def sweep(q, lo, hi, acc, tail, diag):
        z, sp = scores(q, slice(lo, hi))
        row = lax.broadcasted_iota(jnp.int32, (t, ch), 0)
        col = lax.broadcasted_iota(jnp.int32, (t, ch), 1)
        n = (hi - lo) // ch
        for b in reversed(range(n)):
            sl = slice(b * ch, (b + 1) * ch)
            strict = (col < row) if (diag and b == n - 1) else None
            pv, mass = chunk(z[:, sl], sp[:, sl], slice(lo + b * ch, lo + (b + 1) * ch), tail, strict)
            acc = acc + pv
            tail = tail + mass
        return acc, tail

    far_needed = []
    for c in range(n_tiles):
        rows = slice(c * t, (c + 1) * t)
        q = q_ref[0, 0, 0, rows, :]
        last = slice(c * t + ch, (c + 1) * t)
        row = lax.broadcasted_iota(jnp.int32, (ch, ch), 0)
        col = lax.broadcasted_iota(jnp.int32, (ch, ch), 1)
        z_l, sp_l = scores(q[ch:, :], last)
        pv_l, mass_l = chunk(z_l, sp_l, last, None, col < row)
        acc = jnp.concatenate([jnp.zeros((ch, dh), jnp.float32), pv_l], axis=0)
        tail = jnp.concatenate([jnp.zeros((ch, 1), jnp.float32), mass_l], axis=0)
        acc, tail = sweep(q, max(c * t - ch, 0), c * t + ch, acc, tail, True)
        acc_ref[rows, :] = acc
        tail_ref[rows, :] = tail
        far_needed.append(jnp.min(tail) < NEGLIGIBLE_BITS)

    for c in range(n_tiles):
        far_end = c * t - ch
        if far_end <= 0:
            continue
        rows = slice(c * t, (c + 1) * t)

        @pl.when(far_needed[c])
        def _(rows=rows, far_end=far_end):
            acc, _ = sweep(q_ref[0, 0, 0, rows, :], 0, far_end, acc_ref[rows, :], tail_ref[rows, :],
                           False)
            acc_ref[rows, :] = acc

    for c in range(n_tiles):
        rows = slice(c * t, (c + 1) * t)
        _gate_and_store(acc_ref[rows, :], z_ref, o_ref, rows)


def _mixer(qkvz, extra, *, fox, t):
    _, b, h, s, dh = qkvz.shape
    n_tiles = s // t
    full = lambda sec: pl.BlockSpec((1, 1, 1, s, dh), lambda bi, hi: (sec, bi, hi, 0, 0))
    if fox:
        kern = functools.partial(_fox_kernel, t=t, dh=dh, n_tiles=n_tiles)
        extra_spec = pl.BlockSpec((1, 1, 1, s), lambda bi, hi: (bi, hi, 0, 0))
        scratch = [pltpu.VMEM((s, 2 * dh), jnp.bfloat16)]
    else:
        kern = functools.partial(_sb_kernel, t=t, dh=dh, n_tiles=n_tiles)
        extra_spec = pl.BlockSpec(extra.shape, lambda bi, hi: (0, 0))
        scratch = [pltpu.VMEM((s, dh), jnp.float32), pltpu.VMEM((s, 1), jnp.float32)]
    return pl.pallas_call(
        kern,
        grid=(b, h),
        in_specs=[full(0), full(1), full(2), full(3), extra_spec],
        out_specs=pl.BlockSpec((1, s, dh), lambda bi, hi: (bi, 0, hi)),
        out_shape=jax.ShapeDtypeStruct((b, s, h * dh), jnp.bfloat16),
        scratch_shapes=scratch,
        compiler_params=_params("parallel", "parallel"),
        name="fox_mixer" if fox else "sb_mixer",
    )(qkvz, qkvz, qkvz, qkvz, extra)


def _outproj_kernel(g_ref, w_ref, x_ref, lg_ref, lb_ref, o_ref, *, alpha):
    y = jnp.dot(g_ref[0], w_ref[...], preferred_element_type=jnp.float32)
    r = alpha * x_ref[0] + y
    mu = jnp.mean(r, axis=-1, keepdims=True)
    d = r - mu
    var = jnp.mean(d * d, axis=-1, keepdims=True)
    o_ref[0] = d * lax.rsqrt(var + LN_EPS) * lg_ref[...] + lb_ref[...]


def _out_proj_norm(g, w, x, ln_g, ln_b, alpha, tm=512):
    b, s, d = x.shape
    d_inner = g.shape[-1]
    return pl.pallas_call(
        functools.partial(_outproj_kernel, alpha=alpha),
        grid=(b, s // tm),
        in_specs=[pl.BlockSpec((1, tm, d_inner), lambda bi, i: (bi, i, 0)),
                  pl.BlockSpec((d_inner, d), lambda bi, i: (0, 0)),
                  pl.BlockSpec((1, tm, d), lambda bi, i: (bi, i, 0)),
                  pl.BlockSpec((1, d), lambda bi, i: (0, 0)),
                  pl.BlockSpec((1, d), lambda bi, i: (0, 0))],
        out_specs=pl.BlockSpec((1, tm, d), lambda bi, i: (bi, i, 0)),
        out_shape=jax.ShapeDtypeStruct((b, s, d), jnp.float32),
        compiler_params=_params("parallel", "parallel"),
        name="out_proj_norm",
    )(g, w, x, ln_g, ln_b)


def kernel(x, fox_w_in, fox_b_f, fox_w_out, sb_w_in, sb_w_out, ln_g, ln_b):
    b, s, d = x.shape
    depth = ln_g.shape[0]
    n_heads = fox_b_f.shape[1]
    d_inner = fox_w_out.shape[1]
    dh = d_inner // n_heads
    scale = dh ** -0.5
    alpha = (2 * depth) ** 0.25
    t = min(512, s)
    ch = min(MXU_DIM, t)

    u = (jnp.arange(ch)[:, None] >= jnp.arange(ch)[None, :]).astype(jnp.bfloat16)
    for layer in range(depth):
        slot = layer // 2
        fox = layer % 2 == 0
        w_in = fox_w_in[slot] if fox else sb_w_in[slot]
        w_out = fox_w_out[slot] if fox else sb_w_out[slot]
        qkvz = _in_proj(x, w_in[:, :4 * d_inner].astype(jnp.bfloat16), n_heads, dh,
                        scale * LOG2E, tm=min(2048, s), tn=min(1024, d_inner))
        if fox:
            wf = jnp.pad(w_in[:, 4 * d_inner:], ((0, 0), (0, LANES - n_heads)))
            bf = jnp.pad(fox_b_f[slot], (0, LANES - n_heads))[None, :]
            c = _fox_gates(x, wf, bf, ts=min(512, s))
            ck = jnp.transpose(c[:, :, :n_heads], (0, 2, 1))[:, :, None, :]
            g = _mixer(qkvz, ck, fox=True, t=t)
        else:
            g = _mixer(qkvz, u, fox=False, t=t)
        x = _out_proj_norm(g, w_out.astype(jnp.bfloat16), x, ln_g[layer][None, :],
                           ln_b[layer][None, :], alpha, tm=min(512, s))
    return x
```

```python
import functools
import math

import jax
import jax.numpy as jnp
from jax import lax
from jax.experimental import pallas as pl
from jax.experimental.pallas import tpu as pltpu

LANES = 128
MXU_DIM = 256
LN_EPS = 1e-5
NEG_BIG = -1e30
NEGLIGIBLE_BITS = 192.0
LOG2E = math.log2(math.e)
VMEM_LIMIT = 56 * 1024 * 1024

_NT = (((1,), (1,)), ((), ()))


def _params(*sem):
    return pltpu.CompilerParams(dimension_semantics=sem, vmem_limit_bytes=VMEM_LIMIT)


def _split_bf16(a, terms):
    out = []
    for _ in range(terms):
        piece = a.astype(jnp.bfloat16)
        out.append(piece)
        a = a - piece.astype(jnp.float32)
    return out


def _gates_kernel(x_ref, wh_ref, wl_ref, bf_ref, tri_ref, c_ref, carry_ref):
    @pl.when(pl.program_id(1) == 0)
    def _():
        carry_ref[...] = jnp.zeros_like(carry_ref)

    dot = functools.partial(jnp.dot, preferred_element_type=jnp.float32)
    xh, xl = _split_bf16(x_ref[0], 2)
    f = dot(xh, wh_ref[...]) + dot(xh, wl_ref[...]) + dot(xl, wh_ref[...]) + bf_ref[...]
    log_f = jnp.minimum(f, 0.0) - jnp.log(1.0 + jnp.exp(-jnp.abs(f)))
    c = carry_ref[...]
    for piece in _split_bf16(log_f, 3):
        c = c + dot(tri_ref[...], piece)
    c_ref[0] = c
    carry_ref[...] = c[-1:, :]


def _fox_gates(x, wf, bf, ts=512):
    b, s, d = x.shape
    tri = (jnp.arange(ts)[:, None] >= jnp.arange(ts)[None, :]).astype(jnp.bfloat16)
    wh, wl = _split_bf16(wf, 2)
    return pl.pallas_call(
        _gates_kernel,
        grid=(b, s // ts),
        in_specs=[pl.BlockSpec((1, ts, d), lambda i, j: (i, j, 0)),
                  pl.BlockSpec((d, LANES), lambda i, j: (0, 0)),
                  pl.BlockSpec((d, LANES), lambda i, j: (0, 0)),
                  pl.BlockSpec((1, LANES), lambda i, j: (0, 0)),
                  pl.BlockSpec((ts, ts), lambda i, j: (0, 0))],
        out_specs=pl.BlockSpec((1, ts, LANES), lambda i, j: (i, j, 0)),
        out_shape=jax.ShapeDtypeStruct((b, s, LANES), jnp.float32),
        scratch_shapes=[pltpu.VMEM((1, LANES), jnp.float32)],
        compiler_params=_params("parallel", "arbitrary"),
        name="fox_gates",
    )(x, wh, wl, bf, tri)


def _inproj_kernel(x_ref, w_ref, o_ref, xb_ref, *, q_tiles, scale, heads_per_tile, dh):
    n = pl.program_id(2)

    @pl.when(n == 0)
    def _():
        xb_ref[...] = x_ref[0].astype(jnp.bfloat16)

    acc = jnp.dot(xb_ref[...], w_ref[0], preferred_element_type=jnp.float32)
    acc = acc * jnp.where(n < q_tiles, scale, 1.0)
    for hh in range(heads_per_tile):
        o_ref[0, 0, hh] = acc[:, hh * dh:(hh + 1) * dh].astype(o_ref.dtype)


def _in_proj(x, w, slot, n_heads, dh, scale, tm=1024, tn=512):
    b, s, d = x.shape
    d_inner = n_heads * dh
    tiles_per_section = d_inner // tn
    hpt = tn // dh
    kern = functools.partial(_inproj_kernel, q_tiles=tiles_per_section, scale=scale,
                             heads_per_tile=hpt, dh=dh)
    return pl.pallas_call(
        kern,
        grid=(b, s // tm, 4 * tiles_per_section),
        in_specs=[pl.BlockSpec((1, tm, d), lambda bi, i, n: (bi, i, 0)),
                  pl.BlockSpec((1, d, tn), lambda bi, i, n: (slot, 0, n))],
        out_specs=pl.BlockSpec(
            (1, 1, hpt, tm, dh),
            lambda bi, i, n: (n // tiles_per_section, bi, n % tiles_per_section, i, 0)),
        out_shape=jax.ShapeDtypeStruct((4, b, n_heads, s, dh), jnp.bfloat16),
        scratch_shapes=[pltpu.VMEM((tm, d), jnp.bfloat16)],
        compiler_params=_params("parallel", "parallel", "arbitrary"),
        name="in_proj",
    )(x, w)


def _gate_and_store(o, z_ref, o_ref, rows):
    z = z_ref[0, 0, 0, rows, :].astype(jnp.float32)
    silu = z * (1.0 / (1.0 + jnp.exp(-z)))
    o_ref[0, rows, :] = (o * silu).astype(o_ref.dtype)


def _fox_kernel(q_ref, k_ref, v_ref, z_ref, ck_ref, o_ref, vaug_ref, *, t, dh, n_tiles):
    col = lax.broadcasted_iota(jnp.int32, (v_ref.shape[3], dh), 1)
    vaug_ref[:, :dh] = v_ref[0, 0, 0]
    vaug_ref[:, dh:] = jnp.where(col == 0, 1.0, 0.0).astype(vaug_ref.dtype)

    def body(c):
        rows = slice(c * t, (c + 1) * t)
        q = q_ref[0, 0, 0, rows, :]
        c_tile = ck_ref[0, 0, :, c * t:c * t + 1]
        m = acc = None
        for j in range(c + 1):
            sl = slice(j * t, (j + 1) * t)
            s = lax.dot_general(q, k_ref[0, 0, 0, sl, :], _NT, preferred_element_type=jnp.float32)
            s = s - (ck_ref[0, 0, :, sl] - c_tile) * LOG2E
            if j == c:
                row = lax.broadcasted_iota(jnp.int32, (t, t), 0)
                col = lax.broadcasted_iota(jnp.int32, (t, t), 1)
                s = jnp.where(col <= row, s, NEG_BIG)
            bm = jnp.max(s, axis=-1, keepdims=True)
            m_new = bm if j == 0 else jnp.maximum(m, bm)
            p = jnp.exp2(s - m_new).astype(jnp.bfloat16)
            pv = jnp.dot(p, vaug_ref[sl, :], preferred_element_type=jnp.float32)
            acc = pv if j == 0 else acc * jnp.exp2(m - m_new) + pv
            m = m_new
        _gate_and_store(acc[:, :dh] / acc[:, dh:dh + 1], z_ref, o_ref, rows)

    for c in range(n_tiles):
        body(c)


def _sb_kernel(q_ref, k_ref, v_ref, z_ref, u_ref, o_ref, acc_ref, tail_ref, *, t, dh, n_tiles):
    ch = u_ref.shape[0]
    assert t == 2 * ch
    u = u_ref[...]

    def scores(q, keys):
        z = lax.dot_general(q, k_ref[0, 0, 0, keys, :], _NT, preferred_element_type=jnp.float32)
        neg_abs = lax.bitcast_convert_type(
            lax.bitcast_convert_type(z, jnp.uint32) | jnp.uint32(0x80000000), jnp.float32)
        return z, jnp.maximum(z, 0.0) + jnp.log2(1.0 + jnp.exp2(neg_abs))

    def chunk(z_b, sp_b, keys, tail, strict):
        if strict is not None:
            sp_b = jnp.where(strict, sp_b, 0.0)
        r = jnp.dot(sp_b.astype(jnp.bfloat16), u, preferred_element_type=jnp.float32)
        e = z_b - r if tail is None else z_b - r - tail
        w = jnp.exp2(e)
        if strict is not None:
            w = jnp.where(strict, w, 0.0)
        pv = jnp.dot(w.astype(jnp.bfloat16), v_ref[0, 0, 0, keys, :],
                     preferred_element_type=jnp.float32)
        return pv, r[:, 0:1]

    def sweep(q, lo, hi, acc, tail, diag):
        z, sp = scores(q, slice(lo, hi))
        row = lax.broadcasted_iota(jnp.int32, (t, ch), 0)
        col = lax.broadcasted_iota(jnp.int32, (t, ch), 1)
        n = (hi - lo) // ch
        for b in reversed(range(n)):
            sl = slice(b * ch, (b + 1) * ch)
            strict = (col < row) if (diag and b == n - 1) else None
            pv, mass = chunk(z[:, sl], sp[:, sl], slice(lo + b * ch, lo + (b + 1) * ch), tail, strict)
            acc = acc + pv
            tail = tail + mass
        return acc, tail

    far_needed = []
    for c in range(n_tiles):
        rows = slice(c * t, (c + 1) * t)
        q = q_ref[0, 0, 0, rows, :]
        last = slice(c * t + ch, (c + 1) * t)
        row = lax.broadcasted_iota(jnp.int32, (ch, ch), 0)
        col = lax.broadcasted_iota(jnp.int32, (ch, ch), 1)
        z_l, sp_l = scores(q[ch:, :], last)
        pv_l, mass_l = chunk(z_l, sp_l, last, None, col < row)
        acc = jnp.concatenate([jnp.zeros((ch, dh), jnp.float32), pv_l], axis=0)
        tail = jnp.concatenate([jnp.zeros((ch, 1), jnp.float32), mass_l], axis=0)
        acc, tail = sweep(q, max(c * t - ch, 0), c * t + ch, acc, tail, True)
        _gate_and_store(acc, z_ref, o_ref, rows)
        if c * t - ch > 0:
            acc_ref[rows, :] = acc
            tail_ref[rows, :] = tail
            far_needed.append(jnp.min(tail) < NEGLIGIBLE_BITS)
        else:
            far_needed.append(None)

    for c in range(n_tiles):
        if far_needed[c] is None:
            continue
        rows = slice(c * t, (c + 1) * t)

        @pl.when(far_needed[c])
        def _(rows=rows, far_end=c * t - ch):
            acc, _ = sweep(q_ref[0, 0, 0, rows, :], 0, far_end, acc_ref[rows, :], tail_ref[rows, :],
                           False)
            _gate_and_store(acc, z_ref, o_ref, rows)


def _mixer(qkvz, extra, *, fox, t):
    _, b, h, s, dh = qkvz.shape
    n_tiles = s // t
    full = lambda sec: pl.BlockSpec((1, 1, 1, s, dh), lambda bi, hi: (sec, bi, hi, 0, 0))
    if fox:
        kern = functools.partial(_fox_kernel, t=t, dh=dh, n_tiles=n_tiles)
        extra_spec = pl.BlockSpec((1, 1, 1, s), lambda bi, hi: (bi, hi, 0, 0))
        scratch = [pltpu.VMEM((s, 2 * dh), jnp.bfloat16)]
    else:
        kern = functools.partial(_sb_kernel, t=t, dh=dh, n_tiles=n_tiles)
        extra_spec = pl.BlockSpec(extra.shape, lambda bi, hi: (0, 0))
        scratch = [pltpu.VMEM((s, dh), jnp.float32), pltpu.VMEM((s, 1), jnp.float32)]
    return pl.pallas_call(
        kern,
        grid=(b, h),
        in_specs=[full(0), full(1), full(2), full(3), extra_spec],
        out_specs=pl.BlockSpec((1, s, dh), lambda bi, hi: (bi, 0, hi)),
        out_shape=jax.ShapeDtypeStruct((b, s, h * dh), jnp.bfloat16),
        scratch_shapes=scratch,
        compiler_params=_params("parallel", "parallel"),
        name="fox_mixer" if fox else "sb_mixer",
    )(qkvz, qkvz, qkvz, qkvz, extra)


def _outproj_kernel(g_ref, w_ref, x_ref, lg_ref, lb_ref, o_ref, *, alpha, parts):
    tm = g_ref.shape[1] // parts
    for part in range(parts):
        rows = slice(part * tm, (part + 1) * tm)
        y = jnp.dot(g_ref[0, rows, :], w_ref[0], preferred_element_type=jnp.float32)
        r = alpha * x_ref[0, rows, :] + y
        mu = jnp.mean(r, axis=-1, keepdims=True)
        d = r - mu
        var = jnp.mean(d * d, axis=-1, keepdims=True)
        o_ref[0, rows, :] = d * lax.rsqrt(var + LN_EPS) * lg_ref[...] + lb_ref[...]


def _out_proj_norm(g, w, slot, x, ln_g, ln_b, alpha, tm=1024, parts=4):
    b, s, d = x.shape
    d_inner = g.shape[-1]
    return pl.pallas_call(
        functools.partial(_outproj_kernel, alpha=alpha, parts=parts),
        grid=(b, s // tm),
        in_specs=[pl.BlockSpec((1, tm, d_inner), lambda bi, i: (bi, i, 0)),
                  pl.BlockSpec((1, d_inner, d), lambda bi, i: (slot, 0, 0)),
                  pl.BlockSpec((1, tm, d), lambda bi, i: (bi, i, 0)),
                  pl.BlockSpec((1, d), lambda bi, i: (0, 0)),
                  pl.BlockSpec((1, d), lambda bi, i: (0, 0))],
        out_specs=pl.BlockSpec((1, tm, d), lambda bi, i: (bi, i, 0)),
        out_shape=jax.ShapeDtypeStruct((b, s, d), jnp.float32),
        compiler_params=_params("parallel", "parallel"),
        name="out_proj_norm",
    )(g, w, x, ln_g, ln_b)


def kernel(x, fox_w_in, fox_b_f, fox_w_out, sb_w_in, sb_w_out, ln_g, ln_b):
    b, s, d = x.shape
    depth = ln_g.shape[0]
    n_heads = fox_b_f.shape[1]
    d_inner = fox_w_out.shape[1]
    dh = d_inner // n_heads
    scale = dh ** -0.5
    alpha = (2 * depth) ** 0.25
    t = min(512, s)
    ch = min(MXU_DIM, t)
    tm_out = min(1024, s)

    u = (jnp.arange(ch)[:, None] >= jnp.arange(ch)[None, :]).astype(jnp.bfloat16)
    w_in_bf16 = {True: fox_w_in.astype(jnp.bfloat16), False: sb_w_in.astype(jnp.bfloat16)}
    w_out_bf16 = {True: fox_w_out.astype(jnp.bfloat16), False: sb_w_out.astype(jnp.bfloat16)}
    for layer in range(depth):
        slot = layer // 2
        fox = layer % 2 == 0
        qkvz = _in_proj(x, w_in_bf16[fox], slot, n_heads, dh, scale * LOG2E,
                        tm=min(2048, s), tn=min(1024, d_inner))
        if fox:
            wf = jnp.pad(fox_w_in[slot][:, 4 * d_inner:], ((0, 0), (0, LANES - n_heads)))
            bf = jnp.pad(fox_b_f[slot], (0, LANES - n_heads))[None, :]
            c = _fox_gates(x, wf, bf, ts=min(512, s))
            ck = jnp.transpose(c[:, :, :n_heads], (0, 2, 1))[:, :, None, :]
            g = _mixer(qkvz, ck, fox=True, t=t)
        else:
            g = _mixer(qkvz, u, fox=False, t=t)
        x = _out_proj_norm(g, w_out_bf16[fox], slot, x, ln_g[layer][None, :], ln_b[layer][None, :],
                           alpha, tm=tm_out, parts=max(tm_out // MXU_DIM, 1))
    return x
```

```python
import functools
import math

import jax
import jax.numpy as jnp
from jax import lax
from jax.experimental import pallas as pl
from jax.experimental.pallas import tpu as pltpu

LANES = 128
MXU_DIM = 256
LN_EPS = 1e-5
NEG_BIG = -1e30
NEGLIGIBLE_BITS = 192.0
FOX_REACH = 2
BF16_SLACK = 1.01
LOG2E = math.log2(math.e)
VMEM_LIMIT = 56 * 1024 * 1024

_NT = (((1,), (1,)), ((), ()))


def _params(*sem):
    return pltpu.CompilerParams(dimension_semantics=sem, vmem_limit_bytes=VMEM_LIMIT)


def _split_bf16(a, terms):
    out = []
    for _ in range(terms):
        piece = a.astype(jnp.bfloat16)
        out.append(piece)
        a = a - piece.astype(jnp.float32)
    return out


def _gates_kernel(x_ref, wh_ref, wl_ref, bf_ref, tri_ref, c_ref, decay_ref, carry_ref, hist_ref):
    j = pl.program_id(1)
    n_blocks = hist_ref.shape[0]

    @pl.when(j == 0)
    def _():
        carry_ref[...] = jnp.zeros_like(carry_ref)
        decay_ref[...] = jnp.full(decay_ref.shape, -NEG_BIG, jnp.float32)

    dot = functools.partial(jnp.dot, preferred_element_type=jnp.float32)
    xh, xl = _split_bf16(x_ref[0], 2)
    f = dot(xh, wh_ref[...]) + dot(xh, wl_ref[...]) + dot(xl, wh_ref[...]) + bf_ref[...]
    log_f = jnp.minimum(f, 0.0) - jnp.log(1.0 + jnp.exp(-jnp.abs(f)))
    c = carry_ref[...]
    for piece in _split_bf16(log_f, 3):
        c = c + dot(tri_ref[...], piece)
    c_ref[0] = c
    carry_ref[...] = c[-1:, :]

    hist_ref[j] = c
    for g in range(1, n_blocks):
        @pl.when(j >= g)
        def _(g=g):
            gap = jnp.min(hist_ref[j - g] - c, axis=0, keepdims=True)
            decay_ref[0, g:g + 1, :] = jnp.minimum(decay_ref[0, g:g + 1, :], gap)


def _fox_gates(x, wf, bf, ts):
    b, s, d = x.shape
    n_blocks = s // ts
    tri = (jnp.arange(ts)[:, None] >= jnp.arange(ts)[None, :]).astype(jnp.bfloat16)
    wh, wl = _split_bf16(wf, 2)
    return pl.pallas_call(
        _gates_kernel,
        grid=(b, n_blocks),
        in_specs=[pl.BlockSpec((1, ts, d), lambda i, j: (i, j, 0)),
                  pl.BlockSpec((d, LANES), lambda i, j: (0, 0)),
                  pl.BlockSpec((d, LANES), lambda i, j: (0, 0)),
                  pl.BlockSpec((1, LANES), lambda i, j: (0, 0)),
                  pl.BlockSpec((ts, ts), lambda i, j: (0, 0))],
        out_specs=[pl.BlockSpec((1, ts, LANES), lambda i, j: (i, j, 0)),
                   pl.BlockSpec((1, n_blocks, LANES), lambda i, j: (i, 0, 0))],
        out_shape=[jax.ShapeDtypeStruct((b, s, LANES), jnp.float32),
                   jax.ShapeDtypeStruct((b, n_blocks, LANES), jnp.float32)],
        scratch_shapes=[pltpu.VMEM((1, LANES), jnp.float32),
                        pltpu.VMEM((n_blocks, ts, LANES), jnp.float32)],
        compiler_params=_params("parallel", "arbitrary"),
        name="fox_gates",
    )(x, wh, wl, bf, tri)


def _inproj_kernel(x_ref, w_ref, o_ref, *rest, q_tiles, scale, heads_per_tile, dh, with_norms):
    xb_ref = rest[-1]
    n = pl.program_id(2)

    @pl.when(n == 0)
    def _():
        xb_ref[...] = x_ref[0].astype(jnp.bfloat16)

    acc = jnp.dot(xb_ref[...], w_ref[0], preferred_element_type=jnp.float32)
    acc = acc * jnp.where(n < q_tiles, scale, 1.0)
    for hh in range(heads_per_tile):
        head = acc[:, hh * dh:(hh + 1) * dh]
        o_ref[0, 0, hh] = head.astype(o_ref.dtype)
        if with_norms:
            top = jnp.max(jnp.sum(head * head, axis=1, keepdims=True), axis=0, keepdims=True)
            rest[0][0, 0, 0, hh:hh + 1, :] = jnp.broadcast_to(top, (1, LANES))


def _in_proj(x, w, slot, n_heads, dh, scale, tm=1024, tn=512, with_norms=False):
    b, s, d = x.shape
    d_inner = n_heads * dh
    tps = d_inner // tn
    hpt = tn // dh
    kern = functools.partial(_inproj_kernel, q_tiles=tps, scale=scale, heads_per_tile=hpt, dh=dh,
                             with_norms=with_norms)
    out_specs = [pl.BlockSpec((1, 1, hpt, tm, dh),
                              lambda bi, i, n: (n // tps, bi, n % tps, i, 0))]
    out_shape = [jax.ShapeDtypeStruct((4, b, n_heads, s, dh), jnp.bfloat16)]
    if with_norms:
        out_specs.append(pl.BlockSpec((1, 1, 1, hpt, LANES),
                                      lambda bi, i, n: (n // tps, bi, i, n % tps, 0)))
        out_shape.append(jax.ShapeDtypeStruct((4, b, s // tm, n_heads, LANES), jnp.float32))
    out = pl.pallas_call(
        kern,
        grid=(b, s // tm, 4 * tps),
        in_specs=[pl.BlockSpec((1, tm, d), lambda bi, i, n: (bi, i, 0)),
                  pl.BlockSpec((1, d, tn), lambda bi, i, n: (slot, 0, n))],
        out_specs=out_specs,
        out_shape=out_shape,
        scratch_shapes=[pltpu.VMEM((tm, d), jnp.bfloat16)],
        compiler_params=_params("parallel", "parallel", "arbitrary"),
        name="in_proj",
    )(x, w)
    return out if with_norms else out[0]


def _gate_and_store(o, z_ref, o_ref, rows):
    z = z_ref[0, 0, 0, rows, :].astype(jnp.float32)
    silu = z * (1.0 / (1.0 + jnp.exp(-z)))
    o_ref[0, rows, :] = (o * silu).astype(o_ref.dtype)


def _fox_kernel(far_ref, q_ref, k_ref, v_ref, z_ref, ck_ref, o_ref, vaug_ref, *, t, dh, n_tiles):
    col = lax.broadcasted_iota(jnp.int32, (v_ref.shape[3], dh), 1)
    vaug_ref[:, :dh] = v_ref[0, 0, 0]
    vaug_ref[:, dh:] = jnp.where(col == 0, 1.0, 0.0).astype(vaug_ref.dtype)

    def q_tile(c, reach):
        rows = slice(c * t, (c + 1) * t)
        q = q_ref[0, 0, 0, rows, :]
        c_tile = ck_ref[0, 0, :, c * t:c * t + 1]
        m = acc = None
        for j in range(max(c - reach, 0), c + 1):
            sl = slice(j * t, (j + 1) * t)
            s = lax.dot_general(q, k_ref[0, 0, 0, sl, :], _NT, preferred_element_type=jnp.float32)
            s = s - (ck_ref[0, 0, :, sl] - c_tile) * LOG2E
            if j == c:
                row = lax.broadcasted_iota(jnp.int32, (t, t), 0)
                col = lax.broadcasted_iota(jnp.int32, (t, t), 1)
                s = jnp.where(col <= row, s, NEG_BIG)
            bm = jnp.max(s, axis=-1, keepdims=True)
            m_new = bm if m is None else jnp.maximum(m, bm)
            p = jnp.exp2(s - m_new).astype(jnp.bfloat16)
            pv = jnp.dot(p, vaug_ref[sl, :], preferred_element_type=jnp.float32)
            acc = pv if m is None else acc * jnp.exp2(m - m_new) + pv
            m = m_new
        _gate_and_store(acc[:, :dh] / acc[:, dh:dh + 1], z_ref, o_ref, rows)

    def run(reach):
        for c in range(n_tiles):
            q_tile(c, reach)

    if FOX_REACH >= n_tiles - 1:
        run(n_tiles - 1)
    else:
        far = far_ref[pl.program_id(0), pl.program_id(1)]
        pl.when(far == 0)(functools.partial(run, FOX_REACH))
        pl.when(far != 0)(functools.partial(run, n_tiles - 1))


def _fox_far_flags(norms, decay, n_heads):
    qn = jnp.max(norms[0, :, :, :, 0], axis=1)
    kn = jnp.max(norms[1, :, :, :, 0], axis=1)
    y_bound = 2.0 * BF16_SLACK * jnp.sqrt(qn * kn)
    if decay.shape[1] <= FOX_REACH:
        return jnp.zeros(qn.shape, jnp.int32)
    least = jnp.min(decay[:, FOX_REACH:, :n_heads], axis=1) * LOG2E
    return (y_bound - least > -NEGLIGIBLE_BITS).astype(jnp.int32)


def _sb_kernel(q_ref, k_ref, v_ref, z_ref, u_ref, o_ref, acc_ref, tail_ref, *, t, dh, n_tiles):
    ch = u_ref.shape[0]
    assert t == 2 * ch
    u = u_ref[...]

    def scores(q, keys):
        z = lax.dot_general(q, k_ref[0, 0, 0, keys, :], _NT, preferred_element_type=jnp.float32)
        neg_abs = lax.bitcast_convert_type(
            lax.bitcast_convert_type(z, jnp.uint32) | jnp.uint32(0x80000000), jnp.float32)
        return z, jnp.maximum(z, 0.0) + jnp.log2(1.0 + jnp.exp2(neg_abs))

    def chunk(z_b, sp_b, keys, tail, strict):
        if strict is not None:
            sp_b = jnp.where(strict, sp_b, 0.0)
        r = jnp.dot(sp_b.astype(jnp.bfloat16), u, preferred_element_type=jnp.float32)
        e = z_b - r if tail is None else z_b - r - tail
        w = jnp.exp2(e)
        if strict is not None:
            w = jnp.where(strict, w, 0.0)
        pv = jnp.dot(w.astype(jnp.bfloat16), v_ref[0, 0, 0, keys, :],
                     preferred_element_type=jnp.float32)
        return pv, r[:, 0:1]

    def sweep(q, lo, hi, acc, tail, diag):
        z, sp = scores(q, slice(lo, hi))
        row = lax.broadcasted_iota(jnp.int32, (t, ch), 0)
        col = lax.broadcasted_iota(jnp.int32, (t, ch), 1)
        n = (hi - lo) // ch
        for b in reversed(range(n)):
            sl = slice(b * ch, (b + 1) * ch)
            strict = (col < row) if (diag and b == n - 1) else None
            pv, mass = chunk(z[:, sl], sp[:, sl], slice(lo + b * ch, lo + (b + 1) * ch), tail, strict)
            acc = acc + pv
            tail = tail + mass
        return acc, tail

    far_needed = []
    for c in range(n_tiles):
        rows = slice(c * t, (c + 1) * t)
        q = q_ref[0, 0, 0, rows, :]
        last = slice(c * t + ch, (c + 1) * t)
        row = lax.broadcasted_iota(jnp.int32, (ch, ch), 0)
        col = lax.broadcasted_iota(jnp.int32, (ch, ch), 1)
        z_l, sp_l = scores(q[ch:, :], last)
        pv_l, mass_l = chunk(z_l, sp_l, last, None, col < row)
        acc = jnp.concatenate([jnp.zeros((ch, dh), jnp.float32), pv_l], axis=0)
        tail = jnp.concatenate([jnp.zeros((ch, 1), jnp.float32), mass_l], axis=0)
        acc, tail = sweep(q, max(c * t - ch, 0), c * t + ch, acc, tail, True)
        _gate_and_store(acc, z_ref, o_ref, rows)
        if c * t - ch > 0:
            acc_ref[rows, :] = acc
            tail_ref[rows, :] = tail
            far_needed.append(jnp.min(tail) < NEGLIGIBLE_BITS)
        else:
            far_needed.append(None)

    for c in range(n_tiles):
        if far_needed[c] is None:
            continue
        rows = slice(c * t, (c + 1) * t)

        @pl.when(far_needed[c])
        def _(rows=rows, far_end=c * t - ch):
            acc, _ = sweep(q_ref[0, 0, 0, rows, :], 0, far_end, acc_ref[rows, :], tail_ref[rows, :],
                           False)
            _gate_and_store(acc, z_ref, o_ref, rows)


def _mixer(qkvz, extras, *, fox, t):
    _, b, h, s, dh = qkvz.shape
    n_tiles = s // t
    full = lambda sec: pl.BlockSpec((1, 1, 1, s, dh), lambda bi, hi: (sec, bi, hi, 0, 0))
    blocks = [full(0), full(1), full(2), full(3)]
    if fox:
        kern = functools.partial(_fox_kernel, t=t, dh=dh, n_tiles=n_tiles)
        in_specs = ([pl.BlockSpec(memory_space=pltpu.SMEM)] + blocks
                    + [pl.BlockSpec((1, 1, 1, s), lambda bi, hi: (bi, hi, 0, 0))])
        args = (extras[0], qkvz, qkvz, qkvz, qkvz, extras[1])
        scratch = [pltpu.VMEM((s, 2 * dh), jnp.bfloat16)]
    else:
        kern = functools.partial(_sb_kernel, t=t, dh=dh, n_tiles=n_tiles)
        in_specs = blocks + [pl.BlockSpec(extras[0].shape, lambda bi, hi: (0, 0))]
        args = (qkvz, qkvz, qkvz, qkvz, extras[0])
        scratch = [pltpu.VMEM((s, dh), jnp.float32), pltpu.VMEM((s, 1), jnp.float32)]
    return pl.pallas_call(
        kern,
        grid=(b, h),
        in_specs=in_specs,
        out_specs=pl.BlockSpec((1, s, dh), lambda bi, hi: (bi, 0, hi)),
        out_shape=jax.ShapeDtypeStruct((b, s, h * dh), jnp.bfloat16),
        scratch_shapes=scratch,
        compiler_params=_params("parallel", "parallel"),
        name="fox_mixer" if fox else "sb_mixer",
    )(*args)


def _outproj_kernel(g_ref, w_ref, x_ref, lg_ref, lb_ref, o_ref, *, alpha, parts):
    tm = g_ref.shape[1] // parts
    for part in range(parts):
        rows = slice(part * tm, (part + 1) * tm)
        y = jnp.dot(g_ref[0, rows, :], w_ref[0], preferred_element_type=jnp.float32)
        r = alpha * x_ref[0, rows, :] + y
        mu = jnp.mean(r, axis=-1, keepdims=True)
        d = r - mu
        var = jnp.mean(d * d, axis=-1, keepdims=True)
        o_ref[0, rows, :] = d * lax.rsqrt(var + LN_EPS) * lg_ref[...] + lb_ref[...]


def _out_proj_norm(g, w, slot, x, ln_g, ln_b, alpha, tm=1024, parts=4):
    b, s, d = x.shape
    d_inner = g.shape[-1]
    return pl.pallas_call(
        functools.partial(_outproj_kernel, alpha=alpha, parts=parts),
        grid=(b, s // tm),
        in_specs=[pl.BlockSpec((1, tm, d_inner), lambda bi, i: (bi, i, 0)),
                  pl.BlockSpec((1, d_inner, d), lambda bi, i: (slot, 0, 0)),
                  pl.BlockSpec((1, tm, d), lambda bi, i: (bi, i, 0)),
                  pl.BlockSpec((1, d), lambda bi, i: (0, 0)),
                  pl.BlockSpec((1, d), lambda bi, i: (0, 0))],
        out_specs=pl.BlockSpec((1, tm, d), lambda bi, i: (bi, i, 0)),
        out_shape=jax.ShapeDtypeStruct((b, s, d), jnp.float32),
        compiler_params=_params("parallel", "parallel"),
        name="out_proj_norm",
    )(g, w, x, ln_g, ln_b)


def kernel(x, fox_w_in, fox_b_f, fox_w_out, sb_w_in, sb_w_out, ln_g, ln_b):
    b, s, d = x.shape
    depth = ln_g.shape[0]
    n_heads = fox_b_f.shape[1]
    d_inner = fox_w_out.shape[1]
    dh = d_inner // n_heads
    scale = dh ** -0.5
    alpha = (2 * depth) ** 0.25
    t = min(512, s)
    ch = min(MXU_DIM, t)
    tm_out = min(1024, s)

    u = (jnp.arange(ch)[:, None] >= jnp.arange(ch)[None, :]).astype(jnp.bfloat16)
    w_in_bf16 = {True: fox_w_in.astype(jnp.bfloat16), False: sb_w_in.astype(jnp.bfloat16)}
    w_out_bf16 = {True: fox_w_out.astype(jnp.bfloat16), False: sb_w_out.astype(jnp.bfloat16)}
    for layer in range(depth):
        slot = layer // 2
        fox = layer % 2 == 0
        proj = _in_proj(x, w_in_bf16[fox], slot, n_heads, dh, scale * LOG2E,
                        tm=min(2048, s), tn=min(1024, d_inner), with_norms=fox)
        if fox:
            qkvz, norms = proj
            wf = jnp.pad(fox_w_in[slot][:, 4 * d_inner:], ((0, 0), (0, LANES - n_heads)))
            bf = jnp.pad(fox_b_f[slot], (0, LANES - n_heads))[None, :]
            c, decay = _fox_gates(x, wf, bf, ts=t)
            ck = jnp.transpose(c[:, :, :n_heads], (0, 2, 1))[:, :, None, :]
            g = _mixer(qkvz, (_fox_far_flags(norms, decay, n_heads), ck), fox=True, t=t)
        else:
            g = _mixer(proj, (u,), fox=False, t=t)
        x = _out_proj_norm(g, w_out_bf16[fox], slot, x, ln_g[layer][None, :], ln_b[layer][None, :],
                           alpha, tm=tm_out, parts=max(tm_out // MXU_DIM, 1))
    return x
```

```python
import functools
import math

import jax
import jax.numpy as jnp
from jax import lax
from jax.experimental import pallas as pl
from jax.experimental.pallas import tpu as pltpu

LANES = 128
MXU_DIM = 256
LN_EPS = 1e-5
NEG_BIG = -1e30
NEGLIGIBLE_BITS = 192.0
FOX_REACH = 2
BF16_SLACK = 1.01
LOG2E = math.log2(math.e)
VMEM_LIMIT = 56 * 1024 * 1024

_NT = (((1,), (1,)), ((), ()))


def _params(*sem):
    return pltpu.CompilerParams(dimension_semantics=sem, vmem_limit_bytes=VMEM_LIMIT)


def _split_bf16(a, terms):
    out = []
    for _ in range(terms):
        piece = a.astype(jnp.bfloat16)
        out.append(piece)
        a = a - piece.astype(jnp.float32)
    return out


def _gates_kernel(x_ref, wh_ref, wl_ref, bf_ref, tri_ref, c_ref, decay_ref, carry_ref, hist_ref):
    j = pl.program_id(1)
    n_blocks = hist_ref.shape[0]

    @pl.when(j == 0)
    def _():
        carry_ref[...] = jnp.zeros_like(carry_ref)
        decay_ref[...] = jnp.full(decay_ref.shape, -NEG_BIG, jnp.float32)

    dot = functools.partial(jnp.dot, preferred_element_type=jnp.float32)
    xh, xl = _split_bf16(x_ref[0], 2)
    f = dot(xh, wh_ref[...]) + dot(xh, wl_ref[...]) + dot(xl, wh_ref[...]) + bf_ref[...]
    log_f = jnp.minimum(f, 0.0) - jnp.log(1.0 + jnp.exp(-jnp.abs(f)))
    c = carry_ref[...]
    for piece in _split_bf16(log_f, 3):
        c = c + dot(tri_ref[...], piece)
    c_ref[0] = c
    carry_ref[...] = c[-1:, :]

    hist_ref[j] = c
    for g in range(1, n_blocks):
        @pl.when(j >= g)
        def _(g=g):
            gap = jnp.min(hist_ref[j - g] - c, axis=0, keepdims=True)
            decay_ref[0, g:g + 1, :] = jnp.minimum(decay_ref[0, g:g + 1, :], gap)


def _fox_gates(x, wf, bf, ts):
    b, s, d = x.shape
    n_blocks = s // ts
    tri = (jnp.arange(ts)[:, None] >= jnp.arange(ts)[None, :]).astype(jnp.bfloat16)
    wh, wl = _split_bf16(wf, 2)
    return pl.pallas_call(
        _gates_kernel,
        grid=(b, n_blocks),
        in_specs=[pl.BlockSpec((1, ts, d), lambda i, j: (i, j, 0)),
                  pl.BlockSpec((d, LANES), lambda i, j: (0, 0)),
                  pl.BlockSpec((d, LANES), lambda i, j: (0, 0)),
                  pl.BlockSpec((1, LANES), lambda i, j: (0, 0)),
                  pl.BlockSpec((ts, ts), lambda i, j: (0, 0))],
        out_specs=[pl.BlockSpec((1, ts, LANES), lambda i, j: (i, j, 0)),
                   pl.BlockSpec((1, n_blocks, LANES), lambda i, j: (i, 0, 0))],
        out_shape=[jax.ShapeDtypeStruct((b, s, LANES), jnp.float32),
                   jax.ShapeDtypeStruct((b, n_blocks, LANES), jnp.float32)],
        scratch_shapes=[pltpu.VMEM((1, LANES), jnp.float32),
                        pltpu.VMEM((n_blocks, ts, LANES), jnp.float32)],
        compiler_params=_params("parallel", "arbitrary"),
        name="fox_gates",
    )(x, wh, wl, bf, tri)


def _inproj_kernel(x_ref, w_ref, o_ref, *rest, q_tiles, scale, heads_per_tile, dh, with_norms,
                   w_transposed):
    xb_ref = rest[-1]
    n = pl.program_id(2)

    @pl.when(n == 0)
    def _():
        xb_ref[...] = x_ref[0].astype(jnp.bfloat16)

    if w_transposed:
        acc = lax.dot_general(xb_ref[...], w_ref[0], _NT, preferred_element_type=jnp.float32)
    else:
        acc = jnp.dot(xb_ref[...], w_ref[0], preferred_element_type=jnp.float32)
    acc = acc * jnp.where(n < q_tiles, scale, 1.0)
    for hh in range(heads_per_tile):
        head = acc[:, hh * dh:(hh + 1) * dh]
        o_ref[0, 0, hh] = head.astype(o_ref.dtype)
        if with_norms:
            top = jnp.max(jnp.sum(head * head, axis=1, keepdims=True), axis=0, keepdims=True)
            rest[0][0, 0, 0, hh:hh + 1, :] = jnp.broadcast_to(top, (1, LANES))


def _in_proj(x, w, slot, n_heads, dh, scale, tm=1024, tn=512, with_norms=False,
             w_transposed=False):
    b, s, d = x.shape
    d_inner = n_heads * dh
    tps = d_inner // tn
    hpt = tn // dh
    kern = functools.partial(_inproj_kernel, q_tiles=tps, scale=scale, heads_per_tile=hpt, dh=dh,
                             with_norms=with_norms, w_transposed=w_transposed)
    if w_transposed:
        w_spec = pl.BlockSpec((1, tn, d), lambda bi, i, n: (slot, n, 0))
    else:
        w_spec = pl.BlockSpec((1, d, tn), lambda bi, i, n: (slot, 0, n))
    out_specs = [pl.BlockSpec((1, 1, hpt, tm, dh),
                              lambda bi, i, n: (n // tps, bi, n % tps, i, 0))]
    out_shape = [jax.ShapeDtypeStruct((4, b, n_heads, s, dh), jnp.bfloat16)]
    if with_norms:
        out_specs.append(pl.BlockSpec((1, 1, 1, hpt, LANES),
                                      lambda bi, i, n: (n // tps, bi, i, n % tps, 0)))
        out_shape.append(jax.ShapeDtypeStruct((4, b, s // tm, n_heads, LANES), jnp.float32))
    out = pl.pallas_call(
        kern,
        grid=(b, s // tm, 4 * tps),
        in_specs=[pl.BlockSpec((1, tm, d), lambda bi, i, n: (bi, i, 0)),
                  w_spec],
        out_specs=out_specs,
        out_shape=out_shape,
        scratch_shapes=[pltpu.VMEM((tm, d), jnp.bfloat16)],
        compiler_params=_params("parallel", "parallel", "arbitrary"),
        name="in_proj",
    )(x, w)
    return out if with_norms else out[0]


def _gate_and_store(o, z_ref, o_ref, rows):
    z = z_ref[0, 0, 0, rows, :].astype(jnp.float32)
    silu = z * (1.0 / (1.0 + jnp.exp(-z)))
    o_ref[0, rows, :] = (o * silu).astype(o_ref.dtype)


def _fox_kernel(far_ref, q_ref, k_ref, v_ref, z_ref, ck_ref, o_ref, vaug_ref, *, t, dh, n_tiles):
    col = lax.broadcasted_iota(jnp.int32, (v_ref.shape[3], dh), 1)
    vaug_ref[:, :dh] = v_ref[0, 0, 0]
    vaug_ref[:, dh:] = jnp.where(col == 0, 1.0, 0.0).astype(vaug_ref.dtype)

    def q_tile(c, reach):
        rows = slice(c * t, (c + 1) * t)
        q = q_ref[0, 0, 0, rows, :]
        c_tile = ck_ref[0, 0, :, c * t:c * t + 1]
        m = acc = None
        for j in range(max(c - reach, 0), c + 1):
            sl = slice(j * t, (j + 1) * t)
            s = lax.dot_general(q, k_ref[0, 0, 0, sl, :], _NT, preferred_element_type=jnp.float32)
            s = s - (ck_ref[0, 0, :, sl] - c_tile) * LOG2E
            if j == c:
                row = lax.broadcasted_iota(jnp.int32, (t, t), 0)
                col = lax.broadcasted_iota(jnp.int32, (t, t), 1)
                s = jnp.where(col <= row, s, NEG_BIG)
            bm = jnp.max(s, axis=-1, keepdims=True)
            m_new = bm if m is None else jnp.maximum(m, bm)
            p = jnp.exp2(s - m_new).astype(jnp.bfloat16)
            pv = jnp.dot(p, vaug_ref[sl, :], preferred_element_type=jnp.float32)
            acc = pv if m is None else acc * jnp.exp2(m - m_new) + pv
            m = m_new
        _gate_and_store(acc[:, :dh] / acc[:, dh:dh + 1], z_ref, o_ref, rows)

    def run(reach):
        for c in range(n_tiles):
            q_tile(c, reach)

    if FOX_REACH >= n_tiles - 1:
        run(n_tiles - 1)
    else:
        far = far_ref[pl.program_id(0), pl.program_id(1)]
        pl.when(far == 0)(functools.partial(run, FOX_REACH))
        pl.when(far != 0)(functools.partial(run, n_tiles - 1))


def _fox_far_flags(norms, decay, n_heads):
    qn = jnp.max(norms[0, :, :, :, 0], axis=1)
    kn = jnp.max(norms[1, :, :, :, 0], axis=1)
    y_bound = 2.0 * BF16_SLACK * jnp.sqrt(qn * kn)
    if decay.shape[1] <= FOX_REACH:
        return jnp.zeros(qn.shape, jnp.int32)
    least = jnp.min(decay[:, FOX_REACH:, :n_heads], axis=1) * LOG2E
    return (y_bound - least > -NEGLIGIBLE_BITS).astype(jnp.int32)


def _sb_kernel(q_ref, k_ref, v_ref, z_ref, u_ref, o_ref, acc_ref, tail_ref, *, t, dh, n_tiles):
    ch = u_ref.shape[0]
    assert t == 2 * ch
    u = u_ref[...]

    def scores(q, keys):
        z = lax.dot_general(q, k_ref[0, 0, 0, keys, :], _NT, preferred_element_type=jnp.float32)
        neg_abs = lax.bitcast_convert_type(
            lax.bitcast_convert_type(z, jnp.uint32) | jnp.uint32(0x80000000), jnp.float32)
        return z, jnp.maximum(z, 0.0) + jnp.log2(1.0 + jnp.exp2(neg_abs))

    def chunk(z_b, sp_b, keys, tail, strict):
        if strict is not None:
            sp_b = jnp.where(strict, sp_b, 0.0)
        r = jnp.dot(sp_b.astype(jnp.bfloat16), u, preferred_element_type=jnp.float32)
        e = z_b - r if tail is None else z_b - r - tail
        w = jnp.exp2(e)
        if strict is not None:
            w = jnp.where(strict, w, 0.0)
        pv = jnp.dot(w.astype(jnp.bfloat16), v_ref[0, 0, 0, keys, :],
                     preferred_element_type=jnp.float32)
        return pv, r[:, 0:1]

    def sweep(q, lo, hi, acc, tail, diag):
        z, sp = scores(q, slice(lo, hi))
        row = lax.broadcasted_iota(jnp.int32, (t, ch), 0)
        col = lax.broadcasted_iota(jnp.int32, (t, ch), 1)
        n = (hi - lo) // ch
        for b in reversed(range(n)):
            sl = slice(b * ch, (b + 1) * ch)
            strict = (col < row) if (diag and b == n - 1) else None
            pv, mass = chunk(z[:, sl], sp[:, sl], slice(lo + b * ch, lo + (b + 1) * ch), tail, strict)
            acc = acc + pv
            tail = tail + mass
        return acc, tail

    far_needed = []
    for c in range(n_tiles):
        rows = slice(c * t, (c + 1) * t)
        q = q_ref[0, 0, 0, rows, :]
        last = slice(c * t + ch, (c + 1) * t)
        row = lax.broadcasted_iota(jnp.int32, (ch, ch), 0)
        col = lax.broadcasted_iota(jnp.int32, (ch, ch), 1)
        z_l, sp_l = scores(q[ch:, :], last)
        pv_l, mass_l = chunk(z_l, sp_l, last, None, col < row)
        acc = jnp.concatenate([jnp.zeros((ch, dh), jnp.float32), pv_l], axis=0)
        tail = jnp.concatenate([jnp.zeros((ch, 1), jnp.float32), mass_l], axis=0)
        acc, tail = sweep(q, max(c * t - ch, 0), c * t + ch, acc, tail, True)
        _gate_and_store(acc, z_ref, o_ref, rows)
        if c * t - ch > 0:
            acc_ref[rows, :] = acc
            tail_ref[rows, :] = tail
            far_needed.append(jnp.min(tail) < NEGLIGIBLE_BITS)
        else:
            far_needed.append(None)

    for c in range(n_tiles):
        if far_needed[c] is None:
            continue
        rows = slice(c * t, (c + 1) * t)

        @pl.when(far_needed[c])
        def _(rows=rows, far_end=c * t - ch):
            acc, _ = sweep(q_ref[0, 0, 0, rows, :], 0, far_end, acc_ref[rows, :], tail_ref[rows, :],
                           False)
            _gate_and_store(acc, z_ref, o_ref, rows)


def _mixer(qkvz, extras, *, fox, t):
    _, b, h, s, dh = qkvz.shape
    n_tiles = s // t
    full = lambda sec: pl.BlockSpec((1, 1, 1, s, dh), lambda bi, hi: (sec, bi, hi, 0, 0))
    blocks = [full(0), full(1), full(2), full(3)]
    if fox:
        kern = functools.partial(_fox_kernel, t=t, dh=dh, n_tiles=n_tiles)
        in_specs = ([pl.BlockSpec(memory_space=pltpu.SMEM)] + blocks
                    + [pl.BlockSpec((1, 1, 1, s), lambda bi, hi: (bi, hi, 0, 0))])
        args = (extras[0], qkvz, qkvz, qkvz, qkvz, extras[1])
        scratch = [pltpu.VMEM((s, 2 * dh), jnp.bfloat16)]
    else:
        kern = functools.partial(_sb_kernel, t=t, dh=dh, n_tiles=n_tiles)
        in_specs = blocks + [pl.BlockSpec(extras[0].shape, lambda bi, hi: (0, 0))]
        args = (qkvz, qkvz, qkvz, qkvz, extras[0])
        scratch = [pltpu.VMEM((s, dh), jnp.float32), pltpu.VMEM((s, 1), jnp.float32)]
    return pl.pallas_call(
        kern,
        grid=(b, h),
        in_specs=in_specs,
        out_specs=pl.BlockSpec((1, s, dh), lambda bi, hi: (bi, 0, hi)),
        out_shape=jax.ShapeDtypeStruct((b, s, h * dh), jnp.bfloat16),
        scratch_shapes=scratch,
        compiler_params=_params("parallel", "parallel"),
        name="fox_mixer" if fox else "sb_mixer",
    )(*args)


def _outproj_kernel(g_ref, w_ref, x_ref, lg_ref, lb_ref, o_ref, *, alpha, parts):
    tm = g_ref.shape[1] // parts
    for part in range(parts):
        rows = slice(part * tm, (part + 1) * tm)
        y = jnp.dot(g_ref[0, rows, :], w_ref[0], preferred_element_type=jnp.float32)
        r = alpha * x_ref[0, rows, :] + y
        mu = jnp.mean(r, axis=-1, keepdims=True)
        d = r - mu
        var = jnp.mean(d * d, axis=-1, keepdims=True)
        o_ref[0, rows, :] = d * lax.rsqrt(var + LN_EPS) * lg_ref[...] + lb_ref[...]


def _out_proj_norm(g, w, slot, x, ln_g, ln_b, alpha, tm=1024, parts=4):
    b, s, d = x.shape
    d_inner = g.shape[-1]
    return pl.pallas_call(
        functools.partial(_outproj_kernel, alpha=alpha, parts=parts),
        grid=(b, s // tm),
        in_specs=[pl.BlockSpec((1, tm, d_inner), lambda bi, i: (bi, i, 0)),
                  pl.BlockSpec((1, d_inner, d), lambda bi, i: (slot, 0, 0)),
                  pl.BlockSpec((1, tm, d), lambda bi, i: (bi, i, 0)),
                  pl.BlockSpec((1, d), lambda bi, i: (0, 0)),
                  pl.BlockSpec((1, d), lambda bi, i: (0, 0))],
        out_specs=pl.BlockSpec((1, tm, d), lambda bi, i: (bi, i, 0)),
        out_shape=jax.ShapeDtypeStruct((b, s, d), jnp.float32),
        compiler_params=_params("parallel", "parallel"),
        name="out_proj_norm",
    )(g, w, x, ln_g, ln_b)


def kernel(x, fox_w_in, fox_b_f, fox_w_out, sb_w_in, sb_w_out, ln_g, ln_b):
    b, s, d = x.shape
    depth = ln_g.shape[0]
    n_heads = fox_b_f.shape[1]
    d_inner = fox_w_out.shape[1]
    dh = d_inner // n_heads
    scale = dh ** -0.5
    alpha = (2 * depth) ** 0.25
    t = min(512, s)
    ch = min(MXU_DIM, t)
    tm_out = min(1024, s)

    u = (jnp.arange(ch)[:, None] >= jnp.arange(ch)[None, :]).astype(jnp.bfloat16)
    w_in_bf16 = {True: jnp.swapaxes(fox_w_in, 1, 2).astype(jnp.bfloat16),
                 False: sb_w_in.astype(jnp.bfloat16)}
    w_out_bf16 = {True: fox_w_out.astype(jnp.bfloat16), False: sb_w_out.astype(jnp.bfloat16)}
    for layer in range(depth):
        slot = layer // 2
        fox = layer % 2 == 0
        proj = _in_proj(x, w_in_bf16[fox], slot, n_heads, dh, scale * LOG2E,
                        tm=min(2048, s), tn=min(1024, d_inner), with_norms=fox, w_transposed=fox)
        if fox:
            qkvz, norms = proj
            wf = jnp.pad(lax.slice(fox_w_in, (slot, 0, 4 * d_inner), (slot + 1, d, 4 * d_inner + n_heads))[0],
                         ((0, 0), (0, LANES - n_heads)))
            bf = jnp.pad(fox_b_f[slot], (0, LANES - n_heads))[None, :]
            c, decay = _fox_gates(x, wf, bf, ts=t)
            ck = jnp.transpose(c[:, :, :n_heads], (0, 2, 1))[:, :, None, :]
            g = _mixer(qkvz, (_fox_far_flags(norms, decay, n_heads), ck), fox=True, t=t)
        else:
            g = _mixer(proj, (u,), fox=False, t=t)
        x = _out_proj_norm(g, w_out_bf16[fox], slot, x, ln_g[layer][None, :], ln_b[layer][None, :],
                           alpha, tm=tm_out, parts=max(tm_out // MXU_DIM, 1))
    return x
```

```python
import functools
import math

import jax
import jax.numpy as jnp
from jax import lax
from jax.experimental import pallas as pl
from jax.experimental.pallas import tpu as pltpu

LANES = 128
MXU_DIM = 256
LN_EPS = 1e-5
NEG_BIG = -1e30
NEGLIGIBLE_BITS = 192.0
FOX_REACH = 2
BF16_SLACK = 1.01
LOG2E = math.log2(math.e)
VMEM_LIMIT = 56 * 1024 * 1024

_NT = (((1,), (1,)), ((), ()))


def _params(*sem):
    return pltpu.CompilerParams(dimension_semantics=sem, vmem_limit_bytes=VMEM_LIMIT)


def _split_bf16(a, terms):
    out = []
    for _ in range(terms):
        piece = a.astype(jnp.bfloat16)
        out.append(piece)
        a = a - piece.astype(jnp.float32)
    return out


def _gates_kernel(x_ref, wh_ref, wl_ref, bf_ref, tri_ref, c_ref, decay_ref, carry_ref, hist_ref):
    j = pl.program_id(1)
    n_blocks = hist_ref.shape[0]

    @pl.when(j == 0)
    def _():
        carry_ref[...] = jnp.zeros_like(carry_ref)
        decay_ref[...] = jnp.full(decay_ref.shape, -NEG_BIG, jnp.float32)

    dot = functools.partial(jnp.dot, preferred_element_type=jnp.float32)
    xh, xl = _split_bf16(x_ref[0], 2)
    f = dot(xh, wh_ref[...]) + dot(xh, wl_ref[...]) + dot(xl, wh_ref[...]) + bf_ref[...]
    log_f = jnp.minimum(f, 0.0) - jnp.log(1.0 + jnp.exp(-jnp.abs(f)))
    c = carry_ref[...]
    for piece in _split_bf16(log_f, 3):
        c = c + dot(tri_ref[...], piece)
    c_ref[0] = c
    carry_ref[...] = c[-1:, :]

    hist_ref[j] = c
    for g in range(1, n_blocks):
        @pl.when(j >= g)
        def _(g=g):
            gap = jnp.min(hist_ref[j - g] - c, axis=0, keepdims=True)
            decay_ref[0, g:g + 1, :] = jnp.minimum(decay_ref[0, g:g + 1, :], gap)


def _fox_gates(x, wf, bf, ts):
    b, s, d = x.shape
    n_blocks = s // ts
    tri = (jnp.arange(ts)[:, None] >= jnp.arange(ts)[None, :]).astype(jnp.bfloat16)
    wh, wl = _split_bf16(wf, 2)
    return pl.pallas_call(
        _gates_kernel,
        grid=(b, n_blocks),
        in_specs=[pl.BlockSpec((1, ts, d), lambda i, j: (i, j, 0)),
                  pl.BlockSpec((d, LANES), lambda i, j: (0, 0)),
                  pl.BlockSpec((d, LANES), lambda i, j: (0, 0)),
                  pl.BlockSpec((1, LANES), lambda i, j: (0, 0)),
                  pl.BlockSpec((ts, ts), lambda i, j: (0, 0))],
        out_specs=[pl.BlockSpec((1, ts, LANES), lambda i, j: (i, j, 0)),
                   pl.BlockSpec((1, n_blocks, LANES), lambda i, j: (i, 0, 0))],
        out_shape=[jax.ShapeDtypeStruct((b, s, LANES), jnp.float32),
                   jax.ShapeDtypeStruct((b, n_blocks, LANES), jnp.float32)],
        scratch_shapes=[pltpu.VMEM((1, LANES), jnp.float32),
                        pltpu.VMEM((n_blocks, ts, LANES), jnp.float32)],
        compiler_params=_params("parallel", "arbitrary"),
        name="fox_gates",
    )(x, wh, wl, bf, tri)


def _inproj_kernel(x_ref, w_ref, o_ref, *rest, q_tiles, scale, heads_per_tile, dh, with_norms,
                   w_transposed):
    xb_ref = rest[-1]
    n = pl.program_id(2)

    @pl.when(n == 0)
    def _():
        xb_ref[...] = x_ref[0].astype(jnp.bfloat16)

    if w_transposed:
        acc = lax.dot_general(xb_ref[...], w_ref[0], _NT, preferred_element_type=jnp.float32)
    else:
        acc = jnp.dot(xb_ref[...], w_ref[0], preferred_element_type=jnp.float32)
    acc = acc * jnp.where(n < q_tiles, scale, 1.0)
    for hh in range(heads_per_tile):
        head = acc[:, hh * dh:(hh + 1) * dh]
        o_ref[0, 0, hh] = head.astype(o_ref.dtype)
        if with_norms:
            sq = head * head
            half = sq.shape[0] // 2
            paired = jnp.maximum(sq[:half], sq[half:])
            top = jnp.max(jnp.sum(paired, axis=1, keepdims=True), axis=0, keepdims=True)
            rest[0][0, 0, 0, hh:hh + 1, :] = jnp.broadcast_to(top, (1, LANES))


def _in_proj(x, w, slot, n_heads, dh, scale, tm=1024, tn=512, with_norms=False,
             w_transposed=False):
    b, s, d = x.shape
    d_inner = n_heads * dh
    tps = d_inner // tn
    hpt = tn // dh
    kern = functools.partial(_inproj_kernel, q_tiles=tps, scale=scale, heads_per_tile=hpt, dh=dh,
                             with_norms=with_norms, w_transposed=w_transposed)
    if w_transposed:
        w_spec = pl.BlockSpec((1, tn, d), lambda bi, i, n: (slot, n, 0))
    else:
        w_spec = pl.BlockSpec((1, d, tn), lambda bi, i, n: (slot, 0, n))
    out_specs = [pl.BlockSpec((1, 1, hpt, tm, dh),
                              lambda bi, i, n: (n // tps, bi, n % tps, i, 0))]
    out_shape = [jax.ShapeDtypeStruct((4, b, n_heads, s, dh), jnp.bfloat16)]
    if with_norms:
        out_specs.append(pl.BlockSpec((1, 1, 1, hpt, LANES),
                                      lambda bi, i, n: (n // tps, bi, i, n % tps, 0)))
        out_shape.append(jax.ShapeDtypeStruct((4, b, s // tm, n_heads, LANES), jnp.float32))
    out = pl.pallas_call(
        kern,
        grid=(b, s // tm, 4 * tps),
        in_specs=[pl.BlockSpec((1, tm, d), lambda bi, i, n: (bi, i, 0)),
                  w_spec],
        out_specs=out_specs,
        out_shape=out_shape,
        scratch_shapes=[pltpu.VMEM((tm, d), jnp.bfloat16)],
        compiler_params=_params("parallel", "parallel", "arbitrary"),
        name="in_proj",
    )(x, w)
    return out if with_norms else out[0]


def _gate_and_store(o, z_ref, o_ref, rows):
    z = z_ref[0, 0, 0, rows, :].astype(jnp.float32)
    silu = z * (1.0 / (1.0 + jnp.exp(-z)))
    o_ref[0, rows, :] = (o * silu).astype(o_ref.dtype)


def _fox_kernel(far_ref, q_ref, k_ref, v_ref, z_ref, ck_ref, o_ref, vaug_ref, *, t, dh, n_tiles):
    col = lax.broadcasted_iota(jnp.int32, (v_ref.shape[3], dh), 1)
    vaug_ref[:, :dh] = v_ref[0, 0, 0]
    vaug_ref[:, dh:] = jnp.where(col == 0, 1.0, 0.0).astype(vaug_ref.dtype)

    def q_tile(c, reach):
        rows = slice(c * t, (c + 1) * t)
        q = q_ref[0, 0, 0, rows, :]
        c_tile = ck_ref[0, 0, :, c * t:c * t + 1]
        m = acc = None
        for j in range(max(c - reach, 0), c + 1):
            sl = slice(j * t, (j + 1) * t)
            s = lax.dot_general(q, k_ref[0, 0, 0, sl, :], _NT, preferred_element_type=jnp.float32)
            s = s - (ck_ref[0, 0, :, sl] - c_tile) * LOG2E
            if j == c:
                row = lax.broadcasted_iota(jnp.int32, (t, t), 0)
                col = lax.broadcasted_iota(jnp.int32, (t, t), 1)
                s = jnp.where(col <= row, s, NEG_BIG)
            bm = jnp.max(s, axis=-1, keepdims=True)
            m_new = bm if m is None else jnp.maximum(m, bm)
            p = jnp.exp2(s - m_new).astype(jnp.bfloat16)
            pv = jnp.dot(p, vaug_ref[sl, :], preferred_element_type=jnp.float32)
            acc = pv if m is None else acc * jnp.exp2(m - m_new) + pv
            m = m_new
        _gate_and_store(acc[:, :dh] / acc[:, dh:dh + 1], z_ref, o_ref, rows)

    def run(reach):
        for c in range(n_tiles):
            q_tile(c, reach)

    if FOX_REACH >= n_tiles - 1:
        run(n_tiles - 1)
    else:
        far = far_ref[pl.program_id(0), pl.program_id(1)]
        pl.when(far == 0)(functools.partial(run, FOX_REACH))
        pl.when(far != 0)(functools.partial(run, n_tiles - 1))


def _fox_far_flags(norms, decay, n_heads):
    qn = jnp.max(norms[0, :, :, :, 0], axis=1)
    kn = jnp.max(norms[1, :, :, :, 0], axis=1)
    y_bound = 2.0 * BF16_SLACK * jnp.sqrt(qn * kn)
    if decay.shape[1] <= FOX_REACH:
        return jnp.zeros(qn.shape, jnp.int32)
    least = jnp.min(decay[:, FOX_REACH:, :n_heads], axis=1) * LOG2E
    return (y_bound - least > -NEGLIGIBLE_BITS).astype(jnp.int32)


def _sb_kernel(q_ref, k_ref, v_ref, z_ref, u_ref, o_ref, acc_ref, tail_ref, *, t, dh, n_tiles):
    ch = u_ref.shape[0]
    assert t == ch
    u = u_ref[...]

    def scores(q, keys):
        z = lax.dot_general(q, k_ref[0, 0, 0, keys, :], _NT, preferred_element_type=jnp.float32)
        neg_abs = lax.bitcast_convert_type(
            lax.bitcast_convert_type(z, jnp.uint32) | jnp.uint32(0x80000000), jnp.float32)
        return z, jnp.maximum(z, 0.0) + jnp.log2(1.0 + jnp.exp2(neg_abs))

    def chunk(z_b, sp_b, keys, tail, strict):
        if strict is not None:
            sp_b = jnp.where(strict, sp_b, 0.0)
        r = jnp.dot(sp_b.astype(jnp.bfloat16), u, preferred_element_type=jnp.float32)
        e = z_b - r if tail is None else z_b - r - tail
        w = jnp.exp2(e)
        if strict is not None:
            w = jnp.where(strict, w, 0.0)
        pv = jnp.dot(w.astype(jnp.bfloat16), v_ref[0, 0, 0, keys, :],
                     preferred_element_type=jnp.float32)
        return pv, r[:, 0:1]

    def sweep(q, lo, hi, acc, tail, diag):
        z, sp = scores(q, slice(lo, hi))
        row = lax.broadcasted_iota(jnp.int32, (t, ch), 0)
        col = lax.broadcasted_iota(jnp.int32, (t, ch), 1)
        n = (hi - lo) // ch
        for b in reversed(range(n)):
            sl = slice(b * ch, (b + 1) * ch)
            strict = (col < row) if (diag and b == n - 1) else None
            pv, mass = chunk(z[:, sl], sp[:, sl], slice(lo + b * ch, lo + (b + 1) * ch), tail, strict)
            acc = pv if acc is None else acc + pv
            tail = mass if tail is None else tail + mass
        return acc, tail

    far_needed = []
    for c in range(n_tiles):
        rows = slice(c * t, (c + 1) * t)
        q = q_ref[0, 0, 0, rows, :]
        acc, tail = sweep(q, max(c * t - ch, 0), (c + 1) * t, None, None, True)
        _gate_and_store(acc, z_ref, o_ref, rows)
        if c * t - ch > 0:
            acc_ref[rows, :] = acc
            tail_ref[rows, :] = tail
            far_needed.append(jnp.min(tail) < NEGLIGIBLE_BITS)
        else:
            far_needed.append(None)

    for c in range(n_tiles):
        if far_needed[c] is None:
            continue
        rows = slice(c * t, (c + 1) * t)

        @pl.when(far_needed[c])
        def _(rows=rows, far_end=c * t - ch):
            acc, _ = sweep(q_ref[0, 0, 0, rows, :], 0, far_end, acc_ref[rows, :], tail_ref[rows, :],
                           False)
            _gate_and_store(acc, z_ref, o_ref, rows)


def _mixer(qkvz, extras, *, fox, t):
    _, b, h, s, dh = qkvz.shape
    n_tiles = s // t
    full = lambda sec: pl.BlockSpec((1, 1, 1, s, dh), lambda bi, hi: (sec, bi, hi, 0, 0))
    blocks = [full(0), full(1), full(2), full(3)]
    if fox:
        kern = functools.partial(_fox_kernel, t=t, dh=dh, n_tiles=n_tiles)
        in_specs = ([pl.BlockSpec(memory_space=pltpu.SMEM)] + blocks
                    + [pl.BlockSpec((1, 1, 1, s), lambda bi, hi: (bi, hi, 0, 0))])
        args = (extras[0], qkvz, qkvz, qkvz, qkvz, extras[1])
        scratch = [pltpu.VMEM((s, 2 * dh), jnp.bfloat16)]
    else:
        kern = functools.partial(_sb_kernel, t=t, dh=dh, n_tiles=n_tiles)
        in_specs = blocks + [pl.BlockSpec(extras[0].shape, lambda bi, hi: (0, 0))]
        args = (qkvz, qkvz, qkvz, qkvz, extras[0])
        scratch = [pltpu.VMEM((s, dh), jnp.float32), pltpu.VMEM((s, 1), jnp.float32)]
    return pl.pallas_call(
        kern,
        grid=(b, h),
        in_specs=in_specs,
        out_specs=pl.BlockSpec((1, s, dh), lambda bi, hi: (bi, 0, hi)),
        out_shape=jax.ShapeDtypeStruct((b, s, h * dh), jnp.bfloat16),
        scratch_shapes=scratch,
        compiler_params=_params("parallel", "parallel"),
        name="fox_mixer" if fox else "sb_mixer",
    )(*args)


def _outproj_kernel(g_ref, w_ref, x_ref, lg_ref, lb_ref, o_ref, *, alpha, parts):
    tm = g_ref.shape[1] // parts
    for part in range(parts):
        rows = slice(part * tm, (part + 1) * tm)
        y = jnp.dot(g_ref[0, rows, :], w_ref[0], preferred_element_type=jnp.float32)
        r = alpha * x_ref[0, rows, :] + y
        mu = jnp.mean(r, axis=-1, keepdims=True)
        d = r - mu
        var = jnp.mean(d * d, axis=-1, keepdims=True)
        o_ref[0, rows, :] = d * lax.rsqrt(var + LN_EPS) * lg_ref[...] + lb_ref[...]


def _out_proj_norm(g, w, slot, x, ln_g, ln_b, alpha, tm=1024, parts=4):
    b, s, d = x.shape
    d_inner = g.shape[-1]
    return pl.pallas_call(
        functools.partial(_outproj_kernel, alpha=alpha, parts=parts),
        grid=(b, s // tm),
        in_specs=[pl.BlockSpec((1, tm, d_inner), lambda bi, i: (bi, i, 0)),
                  pl.BlockSpec((1, d_inner, d), lambda bi, i: (slot, 0, 0)),
                  pl.BlockSpec((1, tm, d), lambda bi, i: (bi, i, 0)),
                  pl.BlockSpec((1, d), lambda bi, i: (0, 0)),
                  pl.BlockSpec((1, d), lambda bi, i: (0, 0))],
        out_specs=pl.BlockSpec((1, tm, d), lambda bi, i: (bi, i, 0)),
        out_shape=jax.ShapeDtypeStruct((b, s, d), jnp.float32),
        compiler_params=_params("parallel", "parallel"),
        name="out_proj_norm",
    )(g, w, x, ln_g, ln_b)


def kernel(x, fox_w_in, fox_b_f, fox_w_out, sb_w_in, sb_w_out, ln_g, ln_b):
    b, s, d = x.shape
    depth = ln_g.shape[0]
    n_heads = fox_b_f.shape[1]
    d_inner = fox_w_out.shape[1]
    dh = d_inner // n_heads
    scale = dh ** -0.5
    alpha = (2 * depth) ** 0.25
    t = min(512, s)
    ch = min(MXU_DIM, t)
    tm_out = min(1024, s)

    u = (jnp.arange(ch)[:, None] >= jnp.arange(ch)[None, :]).astype(jnp.bfloat16)
    w_in_bf16 = {True: jnp.swapaxes(fox_w_in, 1, 2).astype(jnp.bfloat16),
                 False: sb_w_in.astype(jnp.bfloat16)}
    w_out_bf16 = {True: fox_w_out.astype(jnp.bfloat16), False: sb_w_out.astype(jnp.bfloat16)}
    for layer in range(depth):
        slot = layer // 2
        fox = layer % 2 == 0
        proj = _in_proj(x, w_in_bf16[fox], slot, n_heads, dh, scale * LOG2E,
                        tm=min(2048, s), tn=min(1024, d_inner), with_norms=fox, w_transposed=fox)
        if fox:
            qkvz, norms = proj
            wf = jnp.pad(lax.slice(fox_w_in, (slot, 0, 4 * d_inner), (slot + 1, d, 4 * d_inner + n_heads))[0],
                         ((0, 0), (0, LANES - n_heads)))
            bf = jnp.pad(fox_b_f[slot], (0, LANES - n_heads))[None, :]
            c, decay = _fox_gates(x, wf, bf, ts=t)
            ck = jnp.transpose(c[:, :, :n_heads], (0, 2, 1))[:, :, None, :]
            g = _mixer(qkvz, (_fox_far_flags(norms, decay, n_heads), ck), fox=True, t=t)
        else:
            g = _mixer(proj, (u,), fox=False, t=ch)
        x = _out_proj_norm(g, w_out_bf16[fox], slot, x, ln_g[layer][None, :], ln_b[layer][None, :],
                           alpha, tm=tm_out, parts=max(tm_out // MXU_DIM, 1))
    return x
```

```python
import functools
import math

import jax
import jax.numpy as jnp
from jax import lax
from jax.experimental import pallas as pl
from jax.experimental.pallas import tpu as pltpu

LANES = 128
MXU_DIM = 256
LN_EPS = 1e-5
NEG_BIG = -1e30
NEGLIGIBLE_BITS = 160.0
FOX_REACH = 2
BF16_SLACK = 1.01
LOG2E = math.log2(math.e)
VMEM_LIMIT = 56 * 1024 * 1024

_NT = (((1,), (1,)), ((), ()))


def _params(*sem):
    return pltpu.CompilerParams(dimension_semantics=sem, vmem_limit_bytes=VMEM_LIMIT)


def _split_bf16(a, terms):
    out = []
    for _ in range(terms):
        piece = a.astype(jnp.bfloat16)
        out.append(piece)
        a = a - piece.astype(jnp.float32)
    return out


def _gates_kernel(x_ref, wh_ref, wl_ref, bf_ref, tri_ref, c_ref, decay_ref, carry_ref, hist_ref):
    j = pl.program_id(1)
    n_blocks = hist_ref.shape[0]

    @pl.when(j == 0)
    def _():
        carry_ref[...] = jnp.zeros_like(carry_ref)
        decay_ref[...] = jnp.full(decay_ref.shape, -NEG_BIG, jnp.float32)

    dot = functools.partial(jnp.dot, preferred_element_type=jnp.float32)
    xh, xl = _split_bf16(x_ref[0], 2)
    f = dot(xh, wh_ref[...]) + dot(xh, wl_ref[...]) + dot(xl, wh_ref[...]) + bf_ref[...]
    log_f = jnp.minimum(f, 0.0) - jnp.log(1.0 + jnp.exp(-jnp.abs(f)))
    c = carry_ref[...]
    for piece in _split_bf16(log_f, 3):
        c = c + dot(tri_ref[...], piece)
    c_ref[0] = c
    carry_ref[...] = c[-1:, :]

    hist_ref[j] = c
    for g in range(1, n_blocks):
        @pl.when(j >= g)
        def _(g=g):
            gap = jnp.min(hist_ref[j - g] - c, axis=0, keepdims=True)
            decay_ref[0, g:g + 1, :] = jnp.minimum(decay_ref[0, g:g + 1, :], gap)


def _fox_gates(x, wf, bf, ts):
    b, s, d = x.shape
    n_blocks = s // ts
    tri = (jnp.arange(ts)[:, None] >= jnp.arange(ts)[None, :]).astype(jnp.bfloat16)
    wh, wl = _split_bf16(wf, 2)
    return pl.pallas_call(
        _gates_kernel,
        grid=(b, n_blocks),
        in_specs=[pl.BlockSpec((1, ts, d), lambda i, j: (i, j, 0)),
                  pl.BlockSpec((d, LANES), lambda i, j: (0, 0)),
                  pl.BlockSpec((d, LANES), lambda i, j: (0, 0)),
                  pl.BlockSpec((1, LANES), lambda i, j: (0, 0)),
                  pl.BlockSpec((ts, ts), lambda i, j: (0, 0))],
        out_specs=[pl.BlockSpec((1, ts, LANES), lambda i, j: (i, j, 0)),
                   pl.BlockSpec((1, n_blocks, LANES), lambda i, j: (i, 0, 0))],
        out_shape=[jax.ShapeDtypeStruct((b, s, LANES), jnp.float32),
                   jax.ShapeDtypeStruct((b, n_blocks, LANES), jnp.float32)],
        scratch_shapes=[pltpu.VMEM((1, LANES), jnp.float32),
                        pltpu.VMEM((n_blocks, ts, LANES), jnp.float32)],
        compiler_params=_params("parallel", "arbitrary"),
        name="fox_gates",
    )(x, wh, wl, bf, tri)


def _inproj_kernel(x_ref, w_ref, o_ref, *rest, q_tiles, scale, heads_per_tile, dh, with_norms,
                   w_transposed):
    xb_ref = rest[-1]
    n = pl.program_id(2)

    @pl.when(n == 0)
    def _():
        xb_ref[...] = x_ref[0].astype(jnp.bfloat16)

    if w_transposed:
        acc = lax.dot_general(xb_ref[...], w_ref[0], _NT, preferred_element_type=jnp.float32)
    else:
        acc = jnp.dot(xb_ref[...], w_ref[0], preferred_element_type=jnp.float32)
    acc = acc * jnp.where(n < q_tiles, scale, 1.0)
    for hh in range(heads_per_tile):
        head = acc[:, hh * dh:(hh + 1) * dh]
        o_ref[0, 0, hh] = head.astype(o_ref.dtype)
        if with_norms:
            sq = head * head
            half = sq.shape[0] // 2
            paired = jnp.maximum(sq[:half], sq[half:])
            top = jnp.max(jnp.sum(paired, axis=1, keepdims=True), axis=0, keepdims=True)
            rest[0][0, 0, 0, hh:hh + 1, :] = jnp.broadcast_to(top, (1, LANES))


def _in_proj(x, w, slot, n_heads, dh, scale, tm=1024, tn=512, with_norms=False,
             w_transposed=False):
    b, s, d = x.shape
    d_inner = n_heads * dh
    tps = d_inner // tn
    hpt = tn // dh
    kern = functools.partial(_inproj_kernel, q_tiles=tps, scale=scale, heads_per_tile=hpt, dh=dh,
                             with_norms=with_norms, w_transposed=w_transposed)
    if w_transposed:
        w_spec = pl.BlockSpec((1, tn, d), lambda bi, i, n: (slot, n, 0))
    else:
        w_spec = pl.BlockSpec((1, d, tn), lambda bi, i, n: (slot, 0, n))
    out_specs = [pl.BlockSpec((1, 1, hpt, tm, dh),
                              lambda bi, i, n: (n // tps, bi, n % tps, i, 0))]
    out_shape = [jax.ShapeDtypeStruct((4, b, n_heads, s, dh), jnp.bfloat16)]
    if with_norms:
        out_specs.append(pl.BlockSpec((1, 1, 1, hpt, LANES),
                                      lambda bi, i, n: (n // tps, bi, i, n % tps, 0)))
        out_shape.append(jax.ShapeDtypeStruct((4, b, s // tm, n_heads, LANES), jnp.float32))
    out = pl.pallas_call(
        kern,
        grid=(b, s // tm, 4 * tps),
        in_specs=[pl.BlockSpec((1, tm, d), lambda bi, i, n: (bi, i, 0)),
                  w_spec],
        out_specs=out_specs,
        out_shape=out_shape,
        scratch_shapes=[pltpu.VMEM((tm, d), jnp.bfloat16)],
        compiler_params=_params("parallel", "parallel", "arbitrary"),
        name="in_proj",
    )(x, w)
    return out if with_norms else out[0]


def _gate_and_store(o, z_ref, o_ref, rows):
    z = z_ref[0, 0, 0, rows, :].astype(jnp.float32)
    silu = z * (1.0 / (1.0 + jnp.exp(-z)))
    o_ref[0, rows, :] = (o * silu).astype(o_ref.dtype)


def _fox_kernel(far_ref, q_ref, k_ref, v_ref, z_ref, ck_ref, o_ref, vaug_ref, *, t, dh, n_tiles):
    col = lax.broadcasted_iota(jnp.int32, (v_ref.shape[3], dh), 1)
    vaug_ref[:, :dh] = v_ref[0, 0, 0]
    vaug_ref[:, dh:] = jnp.where(col == 0, 1.0, 0.0).astype(vaug_ref.dtype)

    def q_tile(c, reach):
        rows = slice(c * t, (c + 1) * t)
        q = q_ref[0, 0, 0, rows, :]
        c_tile = ck_ref[0, 0, :, c * t:c * t + 1]
        m = acc = None
        for j in range(max(c - reach, 0), c + 1):
            sl = slice(j * t, (j + 1) * t)
            s = lax.dot_general(q, k_ref[0, 0, 0, sl, :], _NT, preferred_element_type=jnp.float32)
            s = s - (ck_ref[0, 0, :, sl] - c_tile) * LOG2E
            if j == c:
                row = lax.broadcasted_iota(jnp.int32, (t, t), 0)
                col = lax.broadcasted_iota(jnp.int32, (t, t), 1)
                s = jnp.where(col <= row, s, NEG_BIG)
            bm = jnp.max(s, axis=-1, keepdims=True)
            m_new = bm if m is None else jnp.maximum(m, bm)
            p = jnp.exp2(s - m_new).astype(jnp.bfloat16)
            pv = jnp.dot(p, vaug_ref[sl, :], preferred_element_type=jnp.float32)
            acc = pv if m is None else acc * jnp.exp2(m - m_new) + pv
            m = m_new
        _gate_and_store(acc[:, :dh] / acc[:, dh:dh + 1], z_ref, o_ref, rows)

    def run(reach):
        for c in range(n_tiles):
            q_tile(c, reach)

    if FOX_REACH >= n_tiles - 1:
        run(n_tiles - 1)
    else:
        far = far_ref[pl.program_id(0), pl.program_id(1)]
        pl.when(far == 0)(functools.partial(run, FOX_REACH))
        pl.when(far != 0)(functools.partial(run, n_tiles - 1))


def _fox_far_flags(norms, decay, n_heads):
    qn = jnp.max(norms[0, :, :, :, 0], axis=1)
    kn = jnp.max(norms[1, :, :, :, 0], axis=1)
    y_bound = 2.0 * BF16_SLACK * jnp.sqrt(qn * kn)
    if decay.shape[1] <= FOX_REACH:
        return jnp.zeros(qn.shape, jnp.int32)
    least = jnp.min(decay[:, FOX_REACH:, :n_heads], axis=1) * LOG2E
    return (y_bound - least > -NEGLIGIBLE_BITS).astype(jnp.int32)


def _sb_kernel(q_ref, k_ref, v_ref, z_ref, u_ref, o_ref, acc_ref, tail_ref, *, t, dh, n_tiles):
    ch = u_ref.shape[0]
    assert t == ch
    u = u_ref[...]

    def scores(q, keys):
        z = lax.dot_general(q, k_ref[0, 0, 0, keys, :], _NT, preferred_element_type=jnp.float32)
        neg_abs = lax.bitcast_convert_type(
            lax.bitcast_convert_type(z, jnp.uint32) | jnp.uint32(0x80000000), jnp.float32)
        return z, jnp.maximum(z, 0.0) + jnp.log2(1.0 + jnp.exp2(neg_abs))

    def chunk(z_b, sp_b, keys, tail, strict):
        if strict is not None:
            sp_b = jnp.where(strict, sp_b, 0.0)
        r = jnp.dot(sp_b.astype(jnp.bfloat16), u, preferred_element_type=jnp.float32)
        e = z_b - r if tail is None else z_b - r - tail
        w = jnp.exp2(e)
        if strict is not None:
            w = jnp.where(strict, w, 0.0)
        pv = jnp.dot(w.astype(jnp.bfloat16), v_ref[0, 0, 0, keys, :],
                     preferred_element_type=jnp.float32)
        return pv, r[:, 0:1]

    def sweep(q, lo, hi, acc, tail, diag):
        z, sp = scores(q, slice(lo, hi))
        row = lax.broadcasted_iota(jnp.int32, (t, ch), 0)
        col = lax.broadcasted_iota(jnp.int32, (t, ch), 1)
        n = (hi - lo) // ch
        for b in reversed(range(n)):
            sl = slice(b * ch, (b + 1) * ch)
            strict = (col < row) if (diag and b == n - 1) else None
            pv, mass = chunk(z[:, sl], sp[:, sl], slice(lo + b * ch, lo + (b + 1) * ch), tail, strict)
            acc = pv if acc is None else acc + pv
            tail = mass if tail is None else tail + mass
        return acc, tail

    far_needed = []
    for c in range(n_tiles):
        rows = slice(c * t, (c + 1) * t)
        q = q_ref[0, 0, 0, rows, :]
        acc, tail = sweep(q, max(c * t - ch, 0), (c + 1) * t, None, None, True)
        _gate_and_store(acc, z_ref, o_ref, rows)
        if c * t - ch > 0:
            acc_ref[rows, :] = acc
            tail_ref[rows, :] = tail
            far_needed.append(jnp.min(tail) < NEGLIGIBLE_BITS)
        else:
            far_needed.append(None)

    for c in range(n_tiles):
        if far_needed[c] is None:
            continue
        rows = slice(c * t, (c + 1) * t)

        @pl.when(far_needed[c])
        def _(rows=rows, far_end=c * t - ch):
            acc, _ = sweep(q_ref[0, 0, 0, rows, :], 0, far_end, acc_ref[rows, :], tail_ref[rows, :],
                           False)
            _gate_and_store(acc, z_ref, o_ref, rows)


def _mixer(qkvz, extras, *, fox, t):
    _, b, h, s, dh = qkvz.shape
    n_tiles = s // t
    full = lambda sec: pl.BlockSpec((1, 1, 1, s, dh), lambda bi, hi: (sec, bi, hi, 0, 0))
    blocks = [full(0), full(1), full(2), full(3)]
    if fox:
        kern = functools.partial(_fox_kernel, t=t, dh=dh, n_tiles=n_tiles)
        in_specs = ([pl.BlockSpec(memory_space=pltpu.SMEM)] + blocks
                    + [pl.BlockSpec((1, 1, 1, s), lambda bi, hi: (bi, hi, 0, 0))])
        args = (extras[0], qkvz, qkvz, qkvz, qkvz, extras[1])
        scratch = [pltpu.VMEM((s, 2 * dh), jnp.bfloat16)]
    else:
        kern = functools.partial(_sb_kernel, t=t, dh=dh, n_tiles=n_tiles)
        in_specs = blocks + [pl.BlockSpec(extras[0].shape, lambda bi, hi: (0, 0))]
        args = (qkvz, qkvz, qkvz, qkvz, extras[0])
        scratch = [pltpu.VMEM((s, dh), jnp.float32), pltpu.VMEM((s, 1), jnp.float32)]
    return pl.pallas_call(
        kern,
        grid=(b, h),
        in_specs=in_specs,
        out_specs=pl.BlockSpec((1, s, dh), lambda bi, hi: (bi, 0, hi)),
        out_shape=jax.ShapeDtypeStruct((b, s, h * dh), jnp.bfloat16),
        scratch_shapes=scratch,
        compiler_params=_params("parallel", "parallel"),
        name="fox_mixer" if fox else "sb_mixer",
    )(*args)


def _outproj_kernel(g_ref, w_ref, x_ref, lg_ref, lb_ref, o_ref, *, alpha, parts):
    tm = g_ref.shape[1] // parts
    for part in range(parts):
        rows = slice(part * tm, (part + 1) * tm)
        y = jnp.dot(g_ref[0, rows, :], w_ref[0], preferred_element_type=jnp.float32)
        r = alpha * x_ref[0, rows, :] + y
        mu = jnp.mean(r, axis=-1, keepdims=True)
        d = r - mu
        var = jnp.mean(d * d, axis=-1, keepdims=True)
        o_ref[0, rows, :] = d * lax.rsqrt(var + LN_EPS) * lg_ref[...] + lb_ref[...]


def _out_proj_norm(g, w, slot, x, ln_g, ln_b, alpha, tm=1024, parts=4):
    b, s, d = x.shape
    d_inner = g.shape[-1]
    return pl.pallas_call(
        functools.partial(_outproj_kernel, alpha=alpha, parts=parts),
        grid=(b, s // tm),
        in_specs=[pl.BlockSpec((1, tm, d_inner), lambda bi, i: (bi, i, 0)),
                  pl.BlockSpec((1, d_inner, d), lambda bi, i: (slot, 0, 0)),
                  pl.BlockSpec((1, tm, d), lambda bi, i: (bi, i, 0)),
                  pl.BlockSpec((1, d), lambda bi, i: (0, 0)),
                  pl.BlockSpec((1, d), lambda bi, i: (0, 0))],
        out_specs=pl.BlockSpec((1, tm, d), lambda bi, i: (bi, i, 0)),
        out_shape=jax.ShapeDtypeStruct((b, s, d), jnp.float32),
        compiler_params=_params("parallel", "parallel"),
        name="out_proj_norm",
    )(g, w, x, ln_g, ln_b)


def kernel(x, fox_w_in, fox_b_f, fox_w_out, sb_w_in, sb_w_out, ln_g, ln_b):
    b, s, d = x.shape
    depth = ln_g.shape[0]
    n_heads = fox_b_f.shape[1]
    d_inner = fox_w_out.shape[1]
    dh = d_inner // n_heads
    scale = dh ** -0.5
    alpha = (2 * depth) ** 0.25
    t = min(512, s)
    ch = min(MXU_DIM, t)
    tm_out = min(1024, s)

    u = (jnp.arange(ch)[:, None] >= jnp.arange(ch)[None, :]).astype(jnp.bfloat16)
    w_in_bf16 = {True: jnp.swapaxes(fox_w_in, 1, 2).astype(jnp.bfloat16),
                 False: sb_w_in.astype(jnp.bfloat16)}
    w_out_bf16 = {True: fox_w_out.astype(jnp.bfloat16), False: sb_w_out.astype(jnp.bfloat16)}
    for layer in range(depth):
        slot = layer // 2
        fox = layer % 2 == 0
        proj = _in_proj(x, w_in_bf16[fox], slot, n_heads, dh, scale * LOG2E,
                        tm=min(2048, s), tn=min(1024, d_inner), with_norms=fox, w_transposed=fox)
        if fox:
            qkvz, norms = proj
            wf = jnp.pad(lax.slice(fox_w_in, (slot, 0, 4 * d_inner), (slot + 1, d, 4 * d_inner + n_heads))[0],
                         ((0, 0), (0, LANES - n_heads)))
            bf = jnp.pad(fox_b_f[slot], (0, LANES - n_heads))[None, :]
            c, decay = _fox_gates(x, wf, bf, ts=t)
            ck = jnp.transpose(c[:, :, :n_heads], (0, 2, 1))[:, :, None, :]
            g = _mixer(qkvz, (_fox_far_flags(norms, decay, n_heads), ck), fox=True, t=t)
        else:
            g = _mixer(proj, (u,), fox=False, t=ch)
        x = _out_proj_norm(g, w_out_bf16[fox], slot, x, ln_g[layer][None, :], ln_b[layer][None, :],
                           alpha, tm=tm_out, parts=max(tm_out // MXU_DIM, 1))
    return x
```

```python
import functools
import math

import jax
import jax.numpy as jnp
from jax import lax
from jax.experimental import pallas as pl
from jax.experimental.pallas import tpu as pltpu

LANES = 128
MXU_DIM = 256
LN_EPS = 1e-5
NEG_BIG = -1e30
NEGLIGIBLE_BITS = 160.0
FOX_REACH = 2
BF16_SLACK = 1.01
LOG2E = math.log2(math.e)
VMEM_LIMIT = 56 * 1024 * 1024

_NT = (((1,), (1,)), ((), ()))


def _params(*sem):
    return pltpu.CompilerParams(dimension_semantics=sem, vmem_limit_bytes=VMEM_LIMIT)


def _split_bf16(a, terms):
    out = []
    for _ in range(terms):
        piece = a.astype(jnp.bfloat16)
        out.append(piece)
        a = a - piece.astype(jnp.float32)
    return out


def _gates_kernel(x_ref, wh_ref, wl_ref, bf_ref, tri_ref, c_ref, decay_ref, carry_ref, hist_ref):
    j = pl.program_id(1)
    n_blocks = hist_ref.shape[0]

    @pl.when(j == 0)
    def _():
        carry_ref[...] = jnp.zeros_like(carry_ref)
        decay_ref[...] = jnp.full(decay_ref.shape, -NEG_BIG, jnp.float32)

    dot = functools.partial(jnp.dot, preferred_element_type=jnp.float32)
    xh, xl = _split_bf16(x_ref[0], 2)
    f = dot(xh, wh_ref[...]) + dot(xh, wl_ref[...]) + dot(xl, wh_ref[...]) + bf_ref[...]
    log_f = jnp.minimum(f, 0.0) - jnp.log(1.0 + jnp.exp(-jnp.abs(f)))
    c = carry_ref[...]
    for piece in _split_bf16(log_f, 3):
        c = c + dot(tri_ref[...], piece)
    c_ref[0] = c
    carry_ref[...] = c[-1:, :]

    hist_ref[j] = c
    for g in range(1, n_blocks):
        @pl.when(j >= g)
        def _(g=g):
            gap = jnp.min(hist_ref[j - g] - c, axis=0, keepdims=True)
            decay_ref[0, g:g + 1, :] = jnp.minimum(decay_ref[0, g:g + 1, :], gap)


def _fox_gates(x, wf, bf, ts):
    b, s, d = x.shape
    n_blocks = s // ts
    tri = (jnp.arange(ts)[:, None] >= jnp.arange(ts)[None, :]).astype(jnp.bfloat16)
    wh, wl = _split_bf16(wf, 2)
    return pl.pallas_call(
        _gates_kernel,
        grid=(b, n_blocks),
        in_specs=[pl.BlockSpec((1, ts, d), lambda i, j: (i, j, 0)),
                  pl.BlockSpec((d, LANES), lambda i, j: (0, 0)),
                  pl.BlockSpec((d, LANES), lambda i, j: (0, 0)),
                  pl.BlockSpec((1, LANES), lambda i, j: (0, 0)),
                  pl.BlockSpec((ts, ts), lambda i, j: (0, 0))],
        out_specs=[pl.BlockSpec((1, ts, LANES), lambda i, j: (i, j, 0)),
                   pl.BlockSpec((1, n_blocks, LANES), lambda i, j: (i, 0, 0))],
        out_shape=[jax.ShapeDtypeStruct((b, s, LANES), jnp.float32),
                   jax.ShapeDtypeStruct((b, n_blocks, LANES), jnp.float32)],
        scratch_shapes=[pltpu.VMEM((1, LANES), jnp.float32),
                        pltpu.VMEM((n_blocks, ts, LANES), jnp.float32)],
        compiler_params=_params("parallel", "arbitrary"),
        name="fox_gates",
    )(x, wh, wl, bf, tri)


def _inproj_kernel(x_ref, w_ref, o_ref, *rest, q_tiles, scale, heads_per_tile, dh, with_norms,
                   w_transposed):
    xb_ref = rest[-1]
    n = pl.program_id(2)

    @pl.when(n == 0)
    def _():
        xb_ref[...] = x_ref[0].astype(jnp.bfloat16)

    if w_transposed:
        acc = lax.dot_general(xb_ref[...], w_ref[0], _NT, preferred_element_type=jnp.float32)
    else:
        acc = jnp.dot(xb_ref[...], w_ref[0], preferred_element_type=jnp.float32)
    acc = acc * jnp.where(n < q_tiles, scale, 1.0)
    for hh in range(heads_per_tile):
        head = acc[:, hh * dh:(hh + 1) * dh]
        o_ref[0, 0, hh] = head.astype(o_ref.dtype)
        if with_norms:
            sq = head * head
            half = sq.shape[0] // 2
            paired = jnp.maximum(sq[:half], sq[half:])
            top = jnp.max(jnp.sum(paired, axis=1, keepdims=True), axis=0, keepdims=True)
            rest[0][0, 0, 0, hh:hh + 1, :] = jnp.broadcast_to(top, (1, LANES))


def _in_proj(x, w, slot, n_heads, dh, scale, tm=1024, tn=512, with_norms=False,
             w_transposed=False):
    b, s, d = x.shape
    d_inner = n_heads * dh
    tps = d_inner // tn
    hpt = tn // dh
    kern = functools.partial(_inproj_kernel, q_tiles=tps, scale=scale, heads_per_tile=hpt, dh=dh,
                             with_norms=with_norms, w_transposed=w_transposed)
    if w_transposed:
        w_spec = pl.BlockSpec((1, tn, d), lambda bi, i, n: (slot, n, 0))
    else:
        w_spec = pl.BlockSpec((1, d, tn), lambda bi, i, n: (slot, 0, n))
    out_specs = [pl.BlockSpec((1, 1, hpt, tm, dh),
                              lambda bi, i, n: (n // tps, bi, n % tps, i, 0))]
    out_shape = [jax.ShapeDtypeStruct((4, b, n_heads, s, dh), jnp.bfloat16)]
    if with_norms:
        out_specs.append(pl.BlockSpec((1, 1, 1, hpt, LANES),
                                      lambda bi, i, n: (n // tps, bi, i, n % tps, 0)))
        out_shape.append(jax.ShapeDtypeStruct((4, b, s // tm, n_heads, LANES), jnp.float32))
    out = pl.pallas_call(
        kern,
        grid=(b, s // tm, 4 * tps),
        in_specs=[pl.BlockSpec((1, tm, d), lambda bi, i, n: (bi, i, 0)),
                  w_spec],
        out_specs=out_specs,
        out_shape=out_shape,
        scratch_shapes=[pltpu.VMEM((tm, d), jnp.bfloat16)],
        compiler_params=_params("parallel", "parallel", "arbitrary"),
        name="in_proj",
    )(x, w)
    return out if with_norms else out[0]


def _gate_and_store(o, z_ref, o_ref, rows):
    z = z_ref[0, 0, 0, rows, :].astype(jnp.float32)
    silu = z * (1.0 / (1.0 + jnp.exp(-z)))
    o_ref[0, rows, :] = (o * silu).astype(o_ref.dtype)


def _fox_kernel(far_ref, q_ref, k_ref, v_ref, z_ref, ck_ref, o_ref, vaug_ref, *, t, dh, n_tiles):
    col = lax.broadcasted_iota(jnp.int32, (v_ref.shape[3], dh), 1)
    vaug_ref[:, :dh] = v_ref[0, 0, 0]
    vaug_ref[:, dh:] = jnp.where(col == 0, 1.0, 0.0).astype(vaug_ref.dtype)

    def q_tile(c, reach):
        rows = slice(c * t, (c + 1) * t)
        q = q_ref[0, 0, 0, rows, :]
        c_tile = ck_ref[0, 0, :, c * t:c * t + 1]
        m = acc = None
        for j in range(max(c - reach, 0), c + 1):
            sl = slice(j * t, (j + 1) * t)
            s = lax.dot_general(q, k_ref[0, 0, 0, sl, :], _NT, preferred_element_type=jnp.float32)
            s = s - (ck_ref[0, 0, :, sl] - c_tile) * LOG2E
            if j == c:
                row = lax.broadcasted_iota(jnp.int32, (t, t), 0)
                col = lax.broadcasted_iota(jnp.int32, (t, t), 1)
                s = jnp.where(col <= row, s, NEG_BIG)
            bm = jnp.max(s, axis=-1, keepdims=True)
            m_new = bm if m is None else jnp.maximum(m, bm)
            p = jnp.exp2(s - m_new).astype(jnp.bfloat16)
            pv = jnp.dot(p, vaug_ref[sl, :], preferred_element_type=jnp.float32)
            acc = pv if m is None else acc * jnp.exp2(m - m_new) + pv
            m = m_new
        _gate_and_store(acc[:, :dh] / acc[:, dh:dh + 1], z_ref, o_ref, rows)

    def run(reach):
        for c in range(n_tiles):
            q_tile(c, reach)

    if FOX_REACH >= n_tiles - 1:
        run(n_tiles - 1)
    else:
        far = far_ref[pl.program_id(0), pl.program_id(1)]
        pl.when(far == 0)(functools.partial(run, FOX_REACH))
        pl.when(far != 0)(functools.partial(run, n_tiles - 1))


def _fox_far_flags(norms, decay, n_heads):
    qn = jnp.max(norms[0, :, :, :, 0], axis=1)
    kn = jnp.max(norms[1, :, :, :, 0], axis=1)
    y_bound = 2.0 * BF16_SLACK * jnp.sqrt(qn * kn)
    if decay.shape[1] <= FOX_REACH:
        return jnp.zeros(qn.shape, jnp.int32)
    least = jnp.min(decay[:, FOX_REACH:, :n_heads], axis=1) * LOG2E
    return (y_bound - least > -NEGLIGIBLE_BITS).astype(jnp.int32)


def _sb_kernel(q_ref, k_ref, v_ref, z_ref, u_ref, o_ref, acc_ref, tail_ref, *, t, dh, n_tiles):
    ch = u_ref.shape[0]
    assert t == ch
    u = u_ref[...]

    def scores(q, keys):
        z = lax.dot_general(q, k_ref[0, 0, 0, keys, :], _NT, preferred_element_type=jnp.float32)
        neg_abs = lax.bitcast_convert_type(
            lax.bitcast_convert_type(z, jnp.uint32) | jnp.uint32(0x80000000), jnp.float32)
        return z, jnp.maximum(z, 0.0) + jnp.log2(1.0 + jnp.exp2(neg_abs))

    def chunk(z_b, sp_b, keys, tail, strict):
        if strict is not None:
            sp_b = jnp.where(strict, sp_b, 0.0)
        r = jnp.dot(sp_b.astype(jnp.bfloat16), u, preferred_element_type=jnp.float32)
        e = z_b - r if tail is None else z_b - r - tail
        w = jnp.exp2(e)
        if strict is not None:
            w = jnp.where(strict, w, 0.0)
        pv = jnp.dot(w.astype(jnp.bfloat16), v_ref[0, 0, 0, keys, :],
                     preferred_element_type=jnp.float32)
        return pv, r[:, 0:1]

    def sweep(q, lo, hi, acc, tail, diag):
        z, sp = scores(q, slice(lo, hi))
        row = lax.broadcasted_iota(jnp.int32, (t, ch), 0)
        col = lax.broadcasted_iota(jnp.int32, (t, ch), 1)
        n = (hi - lo) // ch
        for b in reversed(range(n)):
            sl = slice(b * ch, (b + 1) * ch)
            strict = (col < row) if (diag and b == n - 1) else None
            pv, mass = chunk(z[:, sl], sp[:, sl], slice(lo + b * ch, lo + (b + 1) * ch), tail, strict)
            acc = pv if acc is None else acc + pv
            tail = mass if tail is None else tail + mass
        return acc, tail

    far_needed = []
    for c in range(n_tiles):
        rows = slice(c * t, (c + 1) * t)
        q = q_ref[0, 0, 0, rows, :]
        acc, tail = sweep(q, max(c * t - ch, 0), (c + 1) * t, None, None, True)
        _gate_and_store(acc, z_ref, o_ref, rows)
        if c * t - ch > 0:
            acc_ref[rows, :] = acc
            tail_ref[rows, :] = tail
            far_needed.append(jnp.min(tail) < NEGLIGIBLE_BITS)
        else:
            far_needed.append(None)

    for c in range(n_tiles):
        if far_needed[c] is None:
            continue
        rows = slice(c * t, (c + 1) * t)

        @pl.when(far_needed[c])
        def _(rows=rows, far_end=c * t - ch):
            acc, _ = sweep(q_ref[0, 0, 0, rows, :], 0, far_end, acc_ref[rows, :], tail_ref[rows, :],
                           False)
            _gate_and_store(acc, z_ref, o_ref, rows)


def _mixer(qkvz, extras, *, fox, t):
    _, b, h, s, dh = qkvz.shape
    n_tiles = s // t
    full = lambda sec: pl.BlockSpec((1, 1, 1, s, dh), lambda bi, hi: (sec, bi, hi, 0, 0))
    blocks = [full(0), full(1), full(2), full(3)]
    if fox:
        kern = functools.partial(_fox_kernel, t=t, dh=dh, n_tiles=n_tiles)
        in_specs = ([pl.BlockSpec(memory_space=pltpu.SMEM)] + blocks
                    + [pl.BlockSpec((1, 1, 1, s), lambda bi, hi: (bi, hi, 0, 0))])
        args = (extras[0], qkvz, qkvz, qkvz, qkvz, extras[1])
        scratch = [pltpu.VMEM((s, 2 * dh), jnp.bfloat16)]
    else:
        kern = functools.partial(_sb_kernel, t=t, dh=dh, n_tiles=n_tiles)
        in_specs = blocks + [pl.BlockSpec(extras[0].shape, lambda bi, hi: (0, 0))]
        args = (qkvz, qkvz, qkvz, qkvz, extras[0])
        scratch = [pltpu.VMEM((s, dh), jnp.float32), pltpu.VMEM((s, 1), jnp.float32)]
    return pl.pallas_call(
        kern,
        grid=(b, h),
        in_specs=in_specs,
        out_specs=pl.BlockSpec((1, s, dh), lambda bi, hi: (bi, 0, hi)),
        out_shape=jax.ShapeDtypeStruct((b, s, h * dh), jnp.bfloat16),
        scratch_shapes=scratch,
        compiler_params=_params("parallel", "parallel"),
        name="fox_mixer" if fox else "sb_mixer",
    )(*args)


def _outproj_kernel(g_ref, w_ref, x_ref, lg_ref, lb_ref, o_ref, *, alpha, parts):
    tm = g_ref.shape[1] // parts
    for part in range(parts):
        rows = slice(part * tm, (part + 1) * tm)
        y = jnp.dot(g_ref[0, rows, :], w_ref[0], preferred_element_type=jnp.float32)
        r = alpha * x_ref[0, rows, :] + y
        mu = jnp.mean(r, axis=-1, keepdims=True)
        d = r - mu
        var = jnp.mean(d * d, axis=-1, keepdims=True)
        o_ref[0, rows, :] = d * lax.rsqrt(var + LN_EPS) * lg_ref[...] + lb_ref[...]


def _out_proj_norm(g, w, slot, x, ln_g, ln_b, alpha, tm=1024, parts=4):
    b, s, d = x.shape
    d_inner = g.shape[-1]
    return pl.pallas_call(
        functools.partial(_outproj_kernel, alpha=alpha, parts=parts),
        grid=(b, s // tm),
        in_specs=[pl.BlockSpec((1, tm, d_inner), lambda bi, i: (bi, i, 0)),
                  pl.BlockSpec((1, d_inner, d), lambda bi, i: (slot, 0, 0)),
                  pl.BlockSpec((1, tm, d), lambda bi, i: (bi, i, 0)),
                  pl.BlockSpec((1, d), lambda bi, i: (0, 0)),
                  pl.BlockSpec((1, d), lambda bi, i: (0, 0))],
        out_specs=pl.BlockSpec((1, tm, d), lambda bi, i: (bi, i, 0)),
        out_shape=jax.ShapeDtypeStruct((b, s, d), jnp.float32),
        compiler_params=_params("parallel", "parallel"),
        name="out_proj_norm",
    )(g, w, x, ln_g, ln_b)


def kernel(x, fox_w_in, fox_b_f, fox_w_out, sb_w_in, sb_w_out, ln_g, ln_b):
    b, s, d = x.shape
    depth = ln_g.shape[0]
    n_heads = fox_b_f.shape[1]
    d_inner = fox_w_out.shape[1]
    dh = d_inner // n_heads
    scale = dh ** -0.5
    alpha = (2 * depth) ** 0.25
    t = min(512, s)
    ch = min(MXU_DIM, t)
    tm_out = min(1024, s)

    u = (jnp.arange(ch)[:, None] >= jnp.arange(ch)[None, :]).astype(jnp.bfloat16)
    w_in_bf16 = {True: jnp.swapaxes(fox_w_in, 1, 2).astype(jnp.bfloat16),
                 False: sb_w_in.astype(jnp.bfloat16)}
    w_out_bf16 = {True: fox_w_out.astype(jnp.bfloat16), False: sb_w_out.astype(jnp.bfloat16)}
    for layer in range(depth):
        slot = layer // 2
        fox = layer % 2 == 0
        proj = _in_proj(x, w_in_bf16[fox], slot, n_heads, dh, scale * LOG2E,
                        tm=min(2048, s), tn=min(1024, d_inner), with_norms=fox, w_transposed=fox)
        if fox:
            qkvz, norms = proj
            wf = lax.slice(fox_w_in, (slot, 0, 4 * d_inner), (slot + 1, d, 4 * d_inner + n_heads))[0]
            wf = jnp.pad(wf, ((0, 0), (0, LANES - n_heads)))
            bf = jnp.pad(fox_b_f[slot], (0, LANES - n_heads))[None, :]
            c, decay = _fox_gates(x, wf, bf, ts=t)
            ck = jnp.transpose(c[:, :, :n_heads], (0, 2, 1))[:, :, None, :]
            g = _mixer(qkvz, (_fox_far_flags(norms, decay, n_heads), ck), fox=True, t=t)
        else:
            g = _mixer(proj, (u,), fox=False, t=ch)
        x = _out_proj_norm(g, w_out_bf16[fox], slot, x, ln_g[layer][None, :], ln_b[layer][None, :],
                           alpha, tm=tm_out, parts=max(tm_out // MXU_DIM, 1))
    return x
```

```python
import functools
import math

import jax
import jax.numpy as jnp
from jax import lax
from jax.experimental import pallas as pl
from jax.experimental.pallas import tpu as pltpu

LANES = 128
MXU_DIM = 256
LN_EPS = 1e-5
NEG_BIG = -1e30
NEGLIGIBLE_BITS = 160.0
FOX_REACH = 2
BF16_SLACK = 1.01
LOG2E = math.log2(math.e)
VMEM_LIMIT = 56 * 1024 * 1024

_NT = (((1,), (1,)), ((), ()))


def _params(*sem):
    return pltpu.CompilerParams(dimension_semantics=sem, vmem_limit_bytes=VMEM_LIMIT)


def _split_bf16(a, terms):
    out = []
    for _ in range(terms):
        piece = a.astype(jnp.bfloat16)
        out.append(piece)
        a = a - piece.astype(jnp.float32)
    return out


def _gates_kernel(x_ref, wh_ref, wl_ref, bf_ref, tri_ref, c_ref, decay_ref, carry_ref, hist_ref):
    j = pl.program_id(1)
    n_blocks = hist_ref.shape[0]

    @pl.when(j == 0)
    def _():
        carry_ref[...] = jnp.zeros_like(carry_ref)
        decay_ref[...] = jnp.full(decay_ref.shape, -NEG_BIG, jnp.float32)

    dot = functools.partial(jnp.dot, preferred_element_type=jnp.float32)
    xh, xl = _split_bf16(x_ref[0], 2)
    f = dot(xh, wh_ref[...]) + dot(xh, wl_ref[...]) + dot(xl, wh_ref[...]) + bf_ref[...]
    log_f = jnp.minimum(f, 0.0) - jnp.log(1.0 + jnp.exp(-jnp.abs(f)))
    c = carry_ref[...]
    for piece in _split_bf16(log_f, 3):
        c = c + dot(tri_ref[...], piece)
    c_ref[0] = c
    carry_ref[...] = c[-1:, :]

    hist_ref[j] = c
    for g in range(1, n_blocks):
        @pl.when(j >= g)
        def _(g=g):
            gap = jnp.min(hist_ref[j - g] - c, axis=0, keepdims=True)
            decay_ref[0, g:g + 1, :] = jnp.minimum(decay_ref[0, g:g + 1, :], gap)


def _fox_gates(x, wf, bf, ts):
    b, s, d = x.shape
    n_blocks = s // ts
    tri = (jnp.arange(ts)[:, None] >= jnp.arange(ts)[None, :]).astype(jnp.bfloat16)
    wh, wl = _split_bf16(wf, 2)
    return pl.pallas_call(
        _gates_kernel,
        grid=(b, n_blocks),
        in_specs=[pl.BlockSpec((1, ts, d), lambda i, j: (i, j, 0)),
                  pl.BlockSpec((d, LANES), lambda i, j: (0, 0)),
                  pl.BlockSpec((d, LANES), lambda i, j: (0, 0)),
                  pl.BlockSpec((1, LANES), lambda i, j: (0, 0)),
                  pl.BlockSpec((ts, ts), lambda i, j: (0, 0))],
        out_specs=[pl.BlockSpec((1, ts, LANES), lambda i, j: (i, j, 0)),
                   pl.BlockSpec((1, n_blocks, LANES), lambda i, j: (i, 0, 0))],
        out_shape=[jax.ShapeDtypeStruct((b, s, LANES), jnp.float32),
                   jax.ShapeDtypeStruct((b, n_blocks, LANES), jnp.float32)],
        scratch_shapes=[pltpu.VMEM((1, LANES), jnp.float32),
                        pltpu.VMEM((n_blocks, ts, LANES), jnp.float32)],
        compiler_params=_params("parallel", "arbitrary"),
        name="fox_gates",
    )(x, wh, wl, bf, tri)


def _inproj_kernel(x_ref, w_ref, o_ref, *rest, q_tiles, scale, heads_per_tile, dh, with_norms,
                   w_transposed):
    xb_ref = rest[-1]
    n = pl.program_id(2)

    @pl.when(n == 0)
    def _():
        xb_ref[...] = x_ref[0].astype(jnp.bfloat16)

    w = w_ref[0].astype(jnp.bfloat16)
    if w_transposed:
        acc = lax.dot_general(xb_ref[...], w, _NT, preferred_element_type=jnp.float32)
    else:
        acc = jnp.dot(xb_ref[...], w, preferred_element_type=jnp.float32)
    acc = acc * jnp.where(n < q_tiles, scale, 1.0)
    for hh in range(heads_per_tile):
        head = acc[:, hh * dh:(hh + 1) * dh]
        o_ref[0, 0, hh] = head.astype(o_ref.dtype)
        if with_norms:
            sq = head * head
            half = sq.shape[0] // 2
            paired = jnp.maximum(sq[:half], sq[half:])
            top = jnp.max(jnp.sum(paired, axis=1, keepdims=True), axis=0, keepdims=True)
            rest[0][0, 0, 0, hh:hh + 1, :] = jnp.broadcast_to(top, (1, LANES))


def _in_proj(x, w, slot, n_heads, dh, scale, tm=1024, tn=512, with_norms=False,
             w_transposed=False):
    b, s, d = x.shape
    d_inner = n_heads * dh
    tps = d_inner // tn
    hpt = tn // dh
    kern = functools.partial(_inproj_kernel, q_tiles=tps, scale=scale, heads_per_tile=hpt, dh=dh,
                             with_norms=with_norms, w_transposed=w_transposed)
    if w_transposed:
        w_spec = pl.BlockSpec((1, tn, d), lambda bi, i, n: (slot, n, 0))
    else:
        w_spec = pl.BlockSpec((1, d, tn), lambda bi, i, n: (slot, 0, n))
    out_specs = [pl.BlockSpec((1, 1, hpt, tm, dh),
                              lambda bi, i, n: (n // tps, bi, n % tps, i, 0))]
    out_shape = [jax.ShapeDtypeStruct((4, b, n_heads, s, dh), jnp.bfloat16)]
    if with_norms:
        out_specs.append(pl.BlockSpec((1, 1, 1, hpt, LANES),
                                      lambda bi, i, n: (n // tps, bi, i, n % tps, 0)))
        out_shape.append(jax.ShapeDtypeStruct((4, b, s // tm, n_heads, LANES), jnp.float32))
    out = pl.pallas_call(
        kern,
        grid=(b, s // tm, 4 * tps),
        in_specs=[pl.BlockSpec((1, tm, d), lambda bi, i, n: (bi, i, 0)),
                  w_spec],
        out_specs=out_specs,
        out_shape=out_shape,
        scratch_shapes=[pltpu.VMEM((tm, d), jnp.bfloat16)],
        compiler_params=_params("parallel", "parallel", "arbitrary"),
        name="in_proj",
    )(x, w)
    return out if with_norms else out[0]


def _gate_and_store(o, z_ref, o_ref, rows):
    z = z_ref[0, 0, 0, rows, :].astype(jnp.float32)
    silu = z * (1.0 / (1.0 + jnp.exp(-z)))
    o_ref[0, rows, :] = (o * silu).astype(o_ref.dtype)


def _fox_kernel(far_ref, q_ref, k_ref, v_ref, z_ref, ck_ref, o_ref, vaug_ref, *, t, dh, n_tiles):
    col = lax.broadcasted_iota(jnp.int32, (v_ref.shape[3], dh), 1)
    vaug_ref[:, :dh] = v_ref[0, 0, 0]
    vaug_ref[:, dh:] = jnp.where(col == 0, 1.0, 0.0).astype(vaug_ref.dtype)

    def q_tile(c, reach):
        rows = slice(c * t, (c + 1) * t)
        q = q_ref[0, 0, 0, rows, :]
        c_tile = ck_ref[0, 0, :, c * t:c * t + 1]
        m = acc = None
        for j in range(max(c - reach, 0), c + 1):
            sl = slice(j * t, (j + 1) * t)
            s = lax.dot_general(q, k_ref[0, 0, 0, sl, :], _NT, preferred_element_type=jnp.float32)
            s = s - (ck_ref[0, 0, :, sl] - c_tile) * LOG2E
            if j == c:
                row = lax.broadcasted_iota(jnp.int32, (t, t), 0)
                col = lax.broadcasted_iota(jnp.int32, (t, t), 1)
                s = jnp.where(col <= row, s, NEG_BIG)
            bm = jnp.max(s, axis=-1, keepdims=True)
            m_new = bm if m is None else jnp.maximum(m, bm)
            p = jnp.exp2(s - m_new).astype(jnp.bfloat16)
            pv = jnp.dot(p, vaug_ref[sl, :], preferred_element_type=jnp.float32)
            acc = pv if m is None else acc * jnp.exp2(m - m_new) + pv
            m = m_new
        _gate_and_store(acc[:, :dh] / acc[:, dh:dh + 1], z_ref, o_ref, rows)

    def run(reach):
        for c in range(n_tiles):
            q_tile(c, reach)

    if FOX_REACH >= n_tiles - 1:
        run(n_tiles - 1)
    else:
        far = far_ref[pl.program_id(0), pl.program_id(1)]
        pl.when(far == 0)(functools.partial(run, FOX_REACH))
        pl.when(far != 0)(functools.partial(run, n_tiles - 1))


def _fox_far_flags(norms, decay, n_heads):
    qn = jnp.max(norms[0, :, :, :, 0], axis=1)
    kn = jnp.max(norms[1, :, :, :, 0], axis=1)
    y_bound = 2.0 * BF16_SLACK * jnp.sqrt(qn * kn)
    if decay.shape[1] <= FOX_REACH:
        return jnp.zeros(qn.shape, jnp.int32)
    least = jnp.min(decay[:, FOX_REACH:, :n_heads], axis=1) * LOG2E
    return (y_bound - least > -NEGLIGIBLE_BITS).astype(jnp.int32)


def _sb_kernel(q_ref, k_ref, v_ref, z_ref, u_ref, o_ref, acc_ref, tail_ref, *, t, dh, n_tiles):
    ch = u_ref.shape[0]
    assert t == ch
    u = u_ref[...]

    def scores(q, keys):
        z = lax.dot_general(q, k_ref[0, 0, 0, keys, :], _NT, preferred_element_type=jnp.float32)
        neg_abs = lax.bitcast_convert_type(
            lax.bitcast_convert_type(z, jnp.uint32) | jnp.uint32(0x80000000), jnp.float32)
        return z, jnp.maximum(z, 0.0) + jnp.log2(1.0 + jnp.exp2(neg_abs))

    def chunk(z_b, sp_b, keys, tail, strict):
        if strict is not None:
            sp_b = jnp.where(strict, sp_b, 0.0)
        r = jnp.dot(sp_b.astype(jnp.bfloat16), u, preferred_element_type=jnp.float32)
        e = z_b - r if tail is None else z_b - r - tail
        w = jnp.exp2(e)
        if strict is not None:
            w = jnp.where(strict, w, 0.0)
        pv = jnp.dot(w.astype(jnp.bfloat16), v_ref[0, 0, 0, keys, :],
                     preferred_element_type=jnp.float32)
        return pv, r[:, 0:1]

    def sweep(q, lo, hi, acc, tail, diag):
        z, sp = scores(q, slice(lo, hi))
        row = lax.broadcasted_iota(jnp.int32, (t, ch), 0)
        col = lax.broadcasted_iota(jnp.int32, (t, ch), 1)
        n = (hi - lo) // ch
        for b in reversed(range(n)):
            sl = slice(b * ch, (b + 1) * ch)
            strict = (col < row) if (diag and b == n - 1) else None
            pv, mass = chunk(z[:, sl], sp[:, sl], slice(lo + b * ch, lo + (b + 1) * ch), tail, strict)
            acc = pv if acc is None else acc + pv
            tail = mass if tail is None else tail + mass
        return acc, tail

    far_needed = []
    for c in range(n_tiles):
        rows = slice(c * t, (c + 1) * t)
        q = q_ref[0, 0, 0, rows, :]
        acc, tail = sweep(q, max(c * t - ch, 0), (c + 1) * t, None, None, True)
        _gate_and_store(acc, z_ref, o_ref, rows)
        if c * t - ch > 0:
            acc_ref[rows, :] = acc
            tail_ref[rows, :] = tail
            far_needed.append(jnp.min(tail) < NEGLIGIBLE_BITS)
        else:
            far_needed.append(None)

    for c in range(n_tiles):
        if far_needed[c] is None:
            continue
        rows = slice(c * t, (c + 1) * t)

        @pl.when(far_needed[c])
        def _(rows=rows, far_end=c * t - ch):
            acc, _ = sweep(q_ref[0, 0, 0, rows, :], 0, far_end, acc_ref[rows, :], tail_ref[rows, :],
                           False)
            _gate_and_store(acc, z_ref, o_ref, rows)


def _mixer(qkvz, extras, *, fox, t):
    _, b, h, s, dh = qkvz.shape
    n_tiles = s // t
    full = lambda sec: pl.BlockSpec((1, 1, 1, s, dh), lambda bi, hi: (sec, bi, hi, 0, 0))
    blocks = [full(0), full(1), full(2), full(3)]
    if fox:
        kern = functools.partial(_fox_kernel, t=t, dh=dh, n_tiles=n_tiles)
        in_specs = ([pl.BlockSpec(memory_space=pltpu.SMEM)] + blocks
                    + [pl.BlockSpec((1, 1, 1, s), lambda bi, hi: (bi, hi, 0, 0))])
        args = (extras[0], qkvz, qkvz, qkvz, qkvz, extras[1])
        scratch = [pltpu.VMEM((s, 2 * dh), jnp.bfloat16)]
    else:
        kern = functools.partial(_sb_kernel, t=t, dh=dh, n_tiles=n_tiles)
        in_specs = blocks + [pl.BlockSpec(extras[0].shape, lambda bi, hi: (0, 0))]
        args = (qkvz, qkvz, qkvz, qkvz, extras[0])
        scratch = [pltpu.VMEM((s, dh), jnp.float32), pltpu.VMEM((s, 1), jnp.float32)]
    return pl.pallas_call(
        kern,
        grid=(b, h),
        in_specs=in_specs,
        out_specs=pl.BlockSpec((1, s, dh), lambda bi, hi: (bi, 0, hi)),
        out_shape=jax.ShapeDtypeStruct((b, s, h * dh), jnp.bfloat16),
        scratch_shapes=scratch,
        compiler_params=_params("parallel", "parallel"),
        name="fox_mixer" if fox else "sb_mixer",
    )(*args)


def _outproj_kernel(g_ref, w_ref, x_ref, lg_ref, lb_ref, o_ref, *, alpha, parts):
    tm = g_ref.shape[1] // parts
    for part in range(parts):
        rows = slice(part * tm, (part + 1) * tm)
        y = jnp.dot(g_ref[0, rows, :], w_ref[0], preferred_element_type=jnp.float32)
        r = alpha * x_ref[0, rows, :] + y
        mu = jnp.mean(r, axis=-1, keepdims=True)
        d = r - mu
        var = jnp.mean(d * d, axis=-1, keepdims=True)
        o_ref[0, rows, :] = d * lax.rsqrt(var + LN_EPS) * lg_ref[...] + lb_ref[...]


def _out_proj_norm(g, w, slot, x, ln_g, ln_b, alpha, tm=1024, parts=4):
    b, s, d = x.shape
    d_inner = g.shape[-1]
    return pl.pallas_call(
        functools.partial(_outproj_kernel, alpha=alpha, parts=parts),
        grid=(b, s // tm),
        in_specs=[pl.BlockSpec((1, tm, d_inner), lambda bi, i: (bi, i, 0)),
                  pl.BlockSpec((1, d_inner, d), lambda bi, i: (slot, 0, 0)),
                  pl.BlockSpec((1, tm, d), lambda bi, i: (bi, i, 0)),
                  pl.BlockSpec((1, d), lambda bi, i: (0, 0)),
                  pl.BlockSpec((1, d), lambda bi, i: (0, 0))],
        out_specs=pl.BlockSpec((1, tm, d), lambda bi, i: (bi, i, 0)),
        out_shape=jax.ShapeDtypeStruct((b, s, d), jnp.float32),
        compiler_params=_params("parallel", "parallel"),
        name="out_proj_norm",
    )(g, w, x, ln_g, ln_b)


def kernel(x, fox_w_in, fox_b_f, fox_w_out, sb_w_in, sb_w_out, ln_g, ln_b):
    b, s, d = x.shape
    depth = ln_g.shape[0]
    n_heads = fox_b_f.shape[1]
    d_inner = fox_w_out.shape[1]
    dh = d_inner // n_heads
    scale = dh ** -0.5
    alpha = (2 * depth) ** 0.25
    t = min(512, s)
    ch = min(MXU_DIM, t)
    tm_out = min(1024, s)

    u = (jnp.arange(ch)[:, None] >= jnp.arange(ch)[None, :]).astype(jnp.bfloat16)
    w_in_bf16 = {True: jnp.swapaxes(fox_w_in, 1, 2), False: sb_w_in}
    w_out_bf16 = {True: fox_w_out.astype(jnp.bfloat16), False: sb_w_out.astype(jnp.bfloat16)}
    for layer in range(depth):
        slot = layer // 2
        fox = layer % 2 == 0
        proj = _in_proj(x, w_in_bf16[fox], slot, n_heads, dh, scale * LOG2E,
                        tm=min(2048, s), tn=min(1024, d_inner), with_norms=fox, w_transposed=fox)
        if fox:
            qkvz, norms = proj
            wf = lax.slice(fox_w_in, (slot, 0, 4 * d_inner), (slot + 1, d, 4 * d_inner + n_heads))[0]
            wf = jnp.pad(wf, ((0, 0), (0, LANES - n_heads)))
            bf = jnp.pad(fox_b_f[slot], (0, LANES - n_heads))[None, :]
            c, decay = _fox_gates(x, wf, bf, ts=t)
            ck = jnp.transpose(c[:, :, :n_heads], (0, 2, 1))[:, :, None, :]
            g = _mixer(qkvz, (_fox_far_flags(norms, decay, n_heads), ck), fox=True, t=t)
        else:
            g = _mixer(proj, (u,), fox=False, t=ch)
        x = _out_proj_norm(g, w_out_bf16[fox], slot, x, ln_g[layer][None, :], ln_b[layer][None, :],
                           alpha, tm=tm_out, parts=max(tm_out // MXU_DIM, 1))
    return x
```

```python
import functools
import math

import jax
import jax.numpy as jnp
from jax import lax
from jax.experimental import pallas as pl
from jax.experimental.pallas import tpu as pltpu

LANES = 128
MXU_DIM = 256
LN_EPS = 1e-5
NEG_BIG = -1e30
NEGLIGIBLE_BITS = 160.0
FOX_REACH = 2
BF16_SLACK = 1.01
LOG2E = math.log2(math.e)
VMEM_LIMIT = 56 * 1024 * 1024

_NT = (((1,), (1,)), ((), ()))


def _params(*sem):
    return pltpu.CompilerParams(dimension_semantics=sem, vmem_limit_bytes=VMEM_LIMIT)


def _split_bf16(a, terms):
    out = []
    for _ in range(terms):
        piece = a.astype(jnp.bfloat16)
        out.append(piece)
        a = a - piece.astype(jnp.float32)
    return out


def _gates_kernel(x_ref, wh_ref, wl_ref, bf_ref, tri_ref, c_ref, decay_ref, carry_ref, hist_ref):
    j = pl.program_id(1)
    n_blocks = hist_ref.shape[0]

    @pl.when(j == 0)
    def _():
        carry_ref[...] = jnp.zeros_like(carry_ref)
        decay_ref[...] = jnp.full(decay_ref.shape, -NEG_BIG, jnp.float32)

    dot = functools.partial(jnp.dot, preferred_element_type=jnp.float32)
    xh, xl = _split_bf16(x_ref[0], 2)
    f = dot(xh, wh_ref[...]) + dot(xh, wl_ref[...]) + dot(xl, wh_ref[...]) + bf_ref[...]
    log_f = jnp.minimum(f, 0.0) - jnp.log(1.0 + jnp.exp(-jnp.abs(f)))
    c = carry_ref[...]
    for piece in _split_bf16(log_f, 3):
        c = c + dot(tri_ref[...], piece)
    c_ref[0] = c
    carry_ref[...] = c[-1:, :]

    hist_ref[j] = c
    for g in range(1, n_blocks):
        @pl.when(j >= g)
        def _(g=g):
            gap = jnp.min(hist_ref[j - g] - c, axis=0, keepdims=True)
            decay_ref[0, g:g + 1, :] = jnp.minimum(decay_ref[0, g:g + 1, :], gap)


def _fox_gates(x, wf, bf, ts):
    b, s, d = x.shape
    n_blocks = s // ts
    tri = (jnp.arange(ts)[:, None] >= jnp.arange(ts)[None, :]).astype(jnp.bfloat16)
    wh, wl = _split_bf16(wf, 2)
    return pl.pallas_call(
        _gates_kernel,
        grid=(b, n_blocks),
        in_specs=[pl.BlockSpec((1, ts, d), lambda i, j: (i, j, 0)),
                  pl.BlockSpec((d, LANES), lambda i, j: (0, 0)),
                  pl.BlockSpec((d, LANES), lambda i, j: (0, 0)),
                  pl.BlockSpec((1, LANES), lambda i, j: (0, 0)),
                  pl.BlockSpec((ts, ts), lambda i, j: (0, 0))],
        out_specs=[pl.BlockSpec((1, ts, LANES), lambda i, j: (i, j, 0)),
                   pl.BlockSpec((1, n_blocks, LANES), lambda i, j: (i, 0, 0))],
        out_shape=[jax.ShapeDtypeStruct((b, s, LANES), jnp.float32),
                   jax.ShapeDtypeStruct((b, n_blocks, LANES), jnp.float32)],
        scratch_shapes=[pltpu.VMEM((1, LANES), jnp.float32),
                        pltpu.VMEM((n_blocks, ts, LANES), jnp.float32)],
        compiler_params=_params("parallel", "arbitrary"),
        name="fox_gates",
    )(x, wh, wl, bf, tri)


def _inproj_kernel(x_ref, w_ref, o_ref, *rest, q_tiles, scale, heads_per_tile, dh, with_norms,
                   w_transposed):
    xb_ref = rest[-1]
    n = pl.program_id(2)

    @pl.when(n == 0)
    def _():
        xb_ref[...] = x_ref[0].astype(jnp.bfloat16)

    w = w_ref[0].astype(jnp.bfloat16)
    if w_transposed:
        acc = lax.dot_general(xb_ref[...], w, _NT, preferred_element_type=jnp.float32)
    else:
        acc = jnp.dot(xb_ref[...], w, preferred_element_type=jnp.float32)
    acc = acc * jnp.where(n < q_tiles, scale, 1.0)
    for hh in range(heads_per_tile):
        head = acc[:, hh * dh:(hh + 1) * dh]
        o_ref[0, 0, hh] = head.astype(o_ref.dtype)
        if with_norms:
            sq = head * head
            half = sq.shape[0] // 2
            paired = jnp.maximum(sq[:half], sq[half:])
            top = jnp.max(jnp.sum(paired, axis=1, keepdims=True), axis=0, keepdims=True)
            rest[0][0, 0, 0, hh:hh + 1, :] = jnp.broadcast_to(top, (1, LANES))


def _in_proj(x, w, slot, n_heads, dh, scale, tm=1024, tn=512, with_norms=False,
             w_transposed=False):
    b, s, d = x.shape
    d_inner = n_heads * dh
    tps = d_inner // tn
    hpt = tn // dh
    kern = functools.partial(_inproj_kernel, q_tiles=tps, scale=scale, heads_per_tile=hpt, dh=dh,
                             with_norms=with_norms, w_transposed=w_transposed)
    if w_transposed:
        w_spec = pl.BlockSpec((1, tn, d), lambda bi, i, n: (slot, n, 0))
    else:
        w_spec = pl.BlockSpec((1, d, tn), lambda bi, i, n: (slot, 0, n))
    out_specs = [pl.BlockSpec((1, 1, hpt, tm, dh),
                              lambda bi, i, n: (n // tps, bi, n % tps, i, 0))]
    out_shape = [jax.ShapeDtypeStruct((4, b, n_heads, s, dh), jnp.bfloat16)]
    if with_norms:
        out_specs.append(pl.BlockSpec((1, 1, 1, hpt, LANES),
                                      lambda bi, i, n: (n // tps, bi, i, n % tps, 0)))
        out_shape.append(jax.ShapeDtypeStruct((4, b, s // tm, n_heads, LANES), jnp.float32))
    out = pl.pallas_call(
        kern,
        grid=(b, s // tm, 4 * tps),
        in_specs=[pl.BlockSpec((1, tm, d), lambda bi, i, n: (bi, i, 0)),
                  w_spec],
        out_specs=out_specs,
        out_shape=out_shape,
        scratch_shapes=[pltpu.VMEM((tm, d), jnp.bfloat16)],
        compiler_params=_params("parallel", "parallel", "arbitrary"),
        name="in_proj",
    )(x, w)
    return out if with_norms else out[0]


def _gate_and_store(o, z_ref, o_ref, rows):
    z = z_ref[0, 0, 0, rows, :].astype(jnp.float32)
    silu = z * (1.0 / (1.0 + jnp.exp(-z)))
    o_ref[0, rows, :] = (o * silu).astype(o_ref.dtype)


def _fox_kernel(far_ref, q_ref, k_ref, v_ref, z_ref, ck_ref, o_ref, vaug_ref, *, t, dh, n_tiles):
    col = lax.broadcasted_iota(jnp.int32, (v_ref.shape[3], dh), 1)
    vaug_ref[:, :dh] = v_ref[0, 0, 0]
    vaug_ref[:, dh:] = jnp.where(col == 0, 1.0, 0.0).astype(vaug_ref.dtype)

    def q_tile(c, reach):
        rows = slice(c * t, (c + 1) * t)
        q = q_ref[0, 0, 0, rows, :]
        c_tile = ck_ref[0, 0, :, c * t:c * t + 1]
        m = acc = None
        for j in range(max(c - reach, 0), c + 1):
            sl = slice(j * t, (j + 1) * t)
            s = lax.dot_general(q, k_ref[0, 0, 0, sl, :], _NT, preferred_element_type=jnp.float32)
            s = s - (ck_ref[0, 0, :, sl] - c_tile) * LOG2E
            if j == c:
                row = lax.broadcasted_iota(jnp.int32, (t, t), 0)
                col = lax.broadcasted_iota(jnp.int32, (t, t), 1)
                s = jnp.where(col <= row, s, NEG_BIG)
            bm = jnp.max(s, axis=-1, keepdims=True)
            m_new = bm if m is None else jnp.maximum(m, bm)
            p = jnp.exp2(s - m_new).astype(jnp.bfloat16)
            pv = jnp.dot(p, vaug_ref[sl, :], preferred_element_type=jnp.float32)
            acc = pv if m is None else acc * jnp.exp2(m - m_new) + pv
            m = m_new
        _gate_and_store(acc[:, :dh] / acc[:, dh:dh + 1], z_ref, o_ref, rows)

    def run(reach):
        for c in range(n_tiles):
            q_tile(c, reach)

    if FOX_REACH >= n_tiles - 1:
        run(n_tiles - 1)
    else:
        far = far_ref[pl.program_id(0), pl.program_id(1)]
        pl.when(far == 0)(functools.partial(run, FOX_REACH))
        pl.when(far != 0)(functools.partial(run, n_tiles - 1))


def _fox_far_flags(norms, decay, n_heads):
    qn = jnp.max(norms[0, :, :, :, 0], axis=1)
    kn = jnp.max(norms[1, :, :, :, 0], axis=1)
    y_bound = 2.0 * BF16_SLACK * jnp.sqrt(qn * kn)
    if decay.shape[1] <= FOX_REACH:
        return jnp.zeros(qn.shape, jnp.int32)
    least = jnp.min(decay[:, FOX_REACH:, :n_heads], axis=1) * LOG2E
    return (y_bound - least > -NEGLIGIBLE_BITS).astype(jnp.int32)


def _sb_kernel(q_ref, k_ref, v_ref, z_ref, u_ref, o_ref, acc_ref, tail_ref, *, t, dh, n_tiles):
    ch = u_ref.shape[0]
    assert t == ch
    u = u_ref[...]

    def scores(q, keys):
        z = lax.dot_general(q, k_ref[0, 0, 0, keys, :], _NT, preferred_element_type=jnp.float32)
        neg_abs = lax.bitcast_convert_type(
            lax.bitcast_convert_type(z, jnp.uint32) | jnp.uint32(0x80000000), jnp.float32)
        return z, jnp.maximum(z, 0.0) + jnp.log2(1.0 + jnp.exp2(neg_abs))

    def chunk(z_b, sp_b, keys, tail, strict):
        if strict is not None:
            sp_b = jnp.where(strict, sp_b, 0.0)
        r = jnp.dot(sp_b.astype(jnp.bfloat16), u, preferred_element_type=jnp.float32)
        e = z_b - r if tail is None else z_b - r - tail
        w = jnp.exp2(e)
        if strict is not None:
            w = jnp.where(strict, w, 0.0)
        pv = jnp.dot(w.astype(jnp.bfloat16), v_ref[0, 0, 0, keys, :],
                     preferred_element_type=jnp.float32)
        return pv, r[:, 0:1]

    def sweep(q, lo, hi, acc, tail, diag):
        z, sp = scores(q, slice(lo, hi))
        row = lax.broadcasted_iota(jnp.int32, (t, ch), 0)
        col = lax.broadcasted_iota(jnp.int32, (t, ch), 1)
        n = (hi - lo) // ch
        for b in reversed(range(n)):
            sl = slice(b * ch, (b + 1) * ch)
            strict = (col < row) if (diag and b == n - 1) else None
            pv, mass = chunk(z[:, sl], sp[:, sl], slice(lo + b * ch, lo + (b + 1) * ch), tail, strict)
            acc = pv if acc is None else acc + pv
            tail = mass if tail is None else tail + mass
        return acc, tail

    far_needed = []
    for c in range(n_tiles):
        rows = slice(c * t, (c + 1) * t)
        q = q_ref[0, 0, 0, rows, :]
        acc, tail = sweep(q, max(c * t - ch, 0), (c + 1) * t, None, None, True)
        _gate_and_store(acc, z_ref, o_ref, rows)
        if c * t - ch > 0:
            acc_ref[rows, :] = acc
            tail_ref[rows, :] = tail
            far_needed.append(jnp.min(tail) < NEGLIGIBLE_BITS)
        else:
            far_needed.append(None)

    for c in range(n_tiles):
        if far_needed[c] is None:
            continue
        rows = slice(c * t, (c + 1) * t)

        @pl.when(far_needed[c])
        def _(rows=rows, far_end=c * t - ch):
            acc, _ = sweep(q_ref[0, 0, 0, rows, :], 0, far_end, acc_ref[rows, :], tail_ref[rows, :],
                           False)
            _gate_and_store(acc, z_ref, o_ref, rows)


def _mixer(qkvz, extras, *, fox, t):
    _, b, h, s, dh = qkvz.shape
    n_tiles = s // t
    full = lambda sec: pl.BlockSpec((1, 1, 1, s, dh), lambda bi, hi: (sec, bi, hi, 0, 0))
    blocks = [full(0), full(1), full(2), full(3)]
    if fox:
        kern = functools.partial(_fox_kernel, t=t, dh=dh, n_tiles=n_tiles)
        in_specs = ([pl.BlockSpec(memory_space=pltpu.SMEM)] + blocks
                    + [pl.BlockSpec((1, 1, 1, s), lambda bi, hi: (bi, hi, 0, 0))])
        args = (extras[0], qkvz, qkvz, qkvz, qkvz, extras[1])
        scratch = [pltpu.VMEM((s, 2 * dh), jnp.bfloat16)]
    else:
        kern = functools.partial(_sb_kernel, t=t, dh=dh, n_tiles=n_tiles)
        in_specs = blocks + [pl.BlockSpec(extras[0].shape, lambda bi, hi: (0, 0))]
        args = (qkvz, qkvz, qkvz, qkvz, extras[0])
        scratch = [pltpu.VMEM((s, dh), jnp.float32), pltpu.VMEM((s, 1), jnp.float32)]
    return pl.pallas_call(
        kern,
        grid=(b, h),
        in_specs=in_specs,
        out_specs=pl.BlockSpec((1, s, dh), lambda bi, hi: (bi, 0, hi)),
        out_shape=jax.ShapeDtypeStruct((b, s, h * dh), jnp.bfloat16),
        scratch_shapes=scratch,
        compiler_params=_params("parallel", "parallel"),
        name="fox_mixer" if fox else "sb_mixer",
    )(*args)


def _outproj_kernel(g_ref, w_ref, x_ref, lg_ref, lb_ref, o_ref, *, alpha, parts):
    tm = g_ref.shape[1] // parts
    for part in range(parts):
        rows = slice(part * tm, (part + 1) * tm)
        y = jnp.dot(g_ref[0, rows, :], w_ref[0], preferred_element_type=jnp.float32)
        r = alpha * x_ref[0, rows, :] + y
        mu = jnp.mean(r, axis=-1, keepdims=True)
        d = r - mu
        var = jnp.mean(d * d, axis=-1, keepdims=True)
        o_ref[0, rows, :] = d * lax.rsqrt(var + LN_EPS) * lg_ref[...] + lb_ref[...]


def _out_proj_norm(g, w, slot, x, ln_g, ln_b, alpha, tm=1024, parts=4):
    b, s, d = x.shape
    d_inner = g.shape[-1]
    return pl.pallas_call(
        functools.partial(_outproj_kernel, alpha=alpha, parts=parts),
        grid=(b, s // tm),
        in_specs=[pl.BlockSpec((1, tm, d_inner), lambda bi, i: (bi, i, 0)),
                  pl.BlockSpec((1, d_inner, d), lambda bi, i: (slot, 0, 0)),
                  pl.BlockSpec((1, tm, d), lambda bi, i: (bi, i, 0)),
                  pl.BlockSpec((1, d), lambda bi, i: (0, 0)),
                  pl.BlockSpec((1, d), lambda bi, i: (0, 0))],
        out_specs=pl.BlockSpec((1, tm, d), lambda bi, i: (bi, i, 0)),
        out_shape=jax.ShapeDtypeStruct((b, s, d), jnp.float32),
        compiler_params=_params("parallel", "parallel"),
        name="out_proj_norm",
    )(g, w, x, ln_g, ln_b)


def kernel(x, fox_w_in, fox_b_f, fox_w_out, sb_w_in, sb_w_out, ln_g, ln_b):
    b, s, d = x.shape
    depth = ln_g.shape[0]
    n_heads = fox_b_f.shape[1]
    d_inner = fox_w_out.shape[1]
    dh = d_inner // n_heads
    scale = dh ** -0.5
    alpha = (2 * depth) ** 0.25
    t = min(512, s)
    ch = min(MXU_DIM, t)
    tm_out = min(1024, s)

    u = (jnp.arange(ch)[:, None] >= jnp.arange(ch)[None, :]).astype(jnp.bfloat16)
    w_in_f32 = {True: jnp.swapaxes(fox_w_in, 1, 2), False: sb_w_in}
    w_out_bf16 = {True: fox_w_out.astype(jnp.bfloat16), False: sb_w_out.astype(jnp.bfloat16)}
    for layer in range(depth):
        slot = layer // 2
        fox = layer % 2 == 0
        proj = _in_proj(x, w_in_f32[fox], slot, n_heads, dh, scale * LOG2E,
                        tm=min(2048, s), tn=min(1024, d_inner), with_norms=fox, w_transposed=fox)
        if fox:
            qkvz, norms = proj
            wf = lax.slice(fox_w_in, (slot, 0, 4 * d_inner), (slot + 1, d, 4 * d_inner + n_heads))[0]
            wf = jnp.pad(wf, ((0, 0), (0, LANES - n_heads)))
            bf = jnp.pad(fox_b_f[slot], (0, LANES - n_heads))[None, :]
            c, decay = _fox_gates(x, wf, bf, ts=t)
            ck = jnp.transpose(c[:, :, :n_heads], (0, 2, 1))[:, :, None, :]
            g = _mixer(qkvz, (_fox_far_flags(norms, decay, n_heads), ck), fox=True, t=t)
        else:
            g = _mixer(proj, (u,), fox=False, t=ch)
        x = _out_proj_norm(g, w_out_bf16[fox], slot, x, ln_g[layer][None, :], ln_b[layer][None, :],
                           alpha, tm=tm_out, parts=max(tm_out // MXU_DIM, 1))
    return x
```
